```python
import jax, jax.numpy as jnp
from jax import lax
import numpy as np


D_MODEL = 1024
BATCH = 32
SEQ = 2048
DEPTH = 4

GLA_HEADS = 4
GLA_VW = D_MODEL // 2
GLA_DV = GLA_VW // GLA_HEADS
GLA_KW = GLA_VW // 2
GLA_DK = GLA_KW // GLA_HEADS
GLA_LORA = 16
GLA_TAU = 16.0
GLA_CHUNK = 64
RWKV_W = D_MODEL - GLA_VW
RWKV_HEAD = 64
RWKV_HEADS = RWKV_W // RWKV_HEAD
RWKV_W_LORA = 32
RWKV_A_LORA = 32
RWKV_G_LORA = 96
RWKV_LNX_EPS = 64e-5
ATT_HEADS = 8
ATT_HEAD = D_MODEL // ATT_HEADS
IDX_HEADS = 8
IDX_DIM = 64
TOPK_MAX = 256
Q_BLOCK = 128
ROPE_THETA = 10000.0
NEG = -1e30
PEER_HEADS = 8
PEER_DKEY = 128
PEER_NKEYS = 128
PEER_N_EXPERTS = PEER_NKEYS * PEER_NKEYS
PEER_TOPK = 16
PEER_TOKEN_BLOCK = 128
DN_ALPHA = (2 * DEPTH) ** 0.25
DN_BETA = (8 * DEPTH) ** -0.25
LN_EPS = 1e-5
N_EVEN = (DEPTH + 1) // 2
N_ODD = DEPTH // 2

GLA_SPLITS = (GLA_KW, GLA_KW, GLA_VW, GLA_VW, GLA_LORA)
RWKV_SPLITS = (RWKV_W, RWKV_W, RWKV_W, RWKV_W_LORA, RWKV_A_LORA, RWKV_G_LORA)
GLA_IN = sum(GLA_SPLITS)
RWKV_IN = sum(RWKV_SPLITS)
EVEN_IN = GLA_IN + RWKV_IN
ODD_SPLITS = (ATT_HEADS * ATT_HEAD, ATT_HEAD, ATT_HEAD, IDX_HEADS * IDX_DIM, IDX_DIM, IDX_HEADS)
ODD_IN = sum(ODD_SPLITS)

kernel_name = 'hybrid_gla_rwkv7_dsa_peer_deepnorm'

F32 = jnp.float32


def _split(t, sizes):
    cuts = [int(c) for c in np.cumsum(sizes)[:-1]]
    return jnp.split(t, cuts, axis=-1)


def layer_norm(x, g, b):
    xf = x.astype(F32)
    mu = xf.mean(-1, keepdims=True)
    var = jnp.square(xf - mu).mean(-1, keepdims=True)
    return ((xf - mu) * lax.rsqrt(var + LN_EPS) * g + b).astype(x.dtype)


def rope_tables(T, dim):
    inv = ROPE_THETA ** (-jnp.arange(0, dim, 2, dtype=F32) / dim)
    ang = jnp.arange(T, dtype=F32)[:, None] * inv[None, :]
    return jnp.cos(ang), jnp.sin(ang)


def apply_rope(x, cos, sin):
    x1, x2 = jnp.split(x, 2, axis=-1)
    return jnp.concatenate([x1 * cos - x2 * sin, x2 * cos + x1 * sin], -1).astype(x.dtype)


def gla_chunked(q, k, v, log_a):
    B, T, H, dk = q.shape
    dv = v.shape[-1]
    C = GLA_CHUNK
    n = T // C

    def chunks(t):
        return t.astype(F32).reshape(B, n, C, H, -1).transpose(0, 3, 1, 2, 4)

    q, k, v, la = chunks(q) * dk ** -0.5, chunks(k), chunks(v), chunks(log_a)
    b = jnp.cumsum(la, axis=3)
    b_last = b[:, :, :, -1:, :]
    q_in = q * jnp.exp(b)
    k_in = k * jnp.exp(-b)
    k_out = k * jnp.exp(b_last - b)
    causal = jnp.tril(jnp.ones((C, C), dtype=bool))
    A = jnp.where(causal, jnp.einsum('bhnid,bhnjd->bhnij', q_in, k_in), 0.0)
    o = jnp.einsum('bhnij,bhnje->bhnie', A, v)
    kv = jnp.einsum('bhnjd,bhnje->bhnde', k_out, v)
    dec = jnp.exp(b_last[:, :, :, 0, :])

    def step(S, inp):
        dec_n, kv_n = inp
        return dec_n[..., None] * S + kv_n, S

    S0 = jnp.zeros((B, H, dk, dv), F32)
    _, S_prev = lax.scan(step, S0, (jnp.moveaxis(dec, 2, 0), jnp.moveaxis(kv, 2, 0)))
    S_prev = jnp.moveaxis(S_prev, 0, 2)
    o = o + jnp.einsum('bhnid,bhnde->bhnie', q_in, S_prev)
    return o.transpose(0, 2, 3, 1, 4).reshape(B, T, H, dv)


def rwkv7_scan(r, w, k, v, a, b):
    B, T, H, N = r.shape

    def step(S, inp):
        r_t, w_t, k_t, v_t, a_t, b_t = inp
        sa = jnp.einsum('bhij,bhj->bhi', S, a_t)
        S = S * w_t[:, :, None, :] + sa[..., None] * b_t[:, :, None, :] + v_t[..., None] * k_t[:, :, None, :]
        return S, jnp.einsum('bhij,bhj->bhi', S, r_t)

    xs = tuple(jnp.moveaxis(t.astype(F32), 1, 0) for t in (r, w, k, v, a, b))
    _, y = lax.scan(step, jnp.zeros((B, H, N, N), F32), xs)
    return jnp.moveaxis(y, 0, 1)


def even_mixer(x, w_in, gla_a_w2, gla_a_b, gla_norm_g, rwkv_mu, rwkv_w0, rwkv_w2, rwkv_a0, rwkv_a2,
               rwkv_g2, rwkv_k_k, rwkv_k_a, rwkv_r_k, rwkv_lnx_g, rwkv_lnx_b, w_out):
    B, T, _ = x.shape

    def heads(t, h):
        return t.reshape(B, T, h, -1)

    p = x @ w_in
    gla_p, rw_p = p[..., :GLA_IN], p[..., GLA_IN:]
    gq, gk, gv, gg, gal = _split(gla_p, GLA_SPLITS)
    log_a = jax.nn.log_sigmoid((gal @ gla_a_w2 + gla_a_b).astype(F32)) / GLA_TAU
    o = gla_chunked(heads(gq, GLA_HEADS), heads(gk, GLA_HEADS), heads(gv, GLA_HEADS), heads(log_a, GLA_HEADS))
    o = o * lax.rsqrt(jnp.mean(jnp.square(o), -1, keepdims=True) + LN_EPS) * gla_norm_g
    gla_out = (o.reshape(B, T, GLA_VW) * jax.nn.silu(gg.astype(F32))).astype(x.dtype)
    prev = jnp.pad(rw_p, ((0, 0), (1, 0), (0, 0)))[:, :-1]
    rw_p = rw_p + (prev - rw_p) * rwkv_mu
    r, k, v, wl, al, gl = _split(rw_p, RWKV_SPLITS)
    w_log = -jax.nn.softplus(-(rwkv_w0 + jnp.tanh(wl) @ rwkv_w2).astype(F32)) - 0.5
    decay = jnp.exp(-jnp.exp(w_log))
    a = jax.nn.sigmoid((rwkv_a0 + al @ rwkv_a2).astype(F32))
    g = jax.nn.sigmoid(gl) @ rwkv_g2
    kk = heads((k * rwkv_k_k).astype(F32), RWKV_HEADS)
    kk = kk / jnp.maximum(jnp.sqrt(jnp.sum(jnp.square(kk), -1, keepdims=True)), 1e-12)
    k = k * (1.0 + (a - 1.0) * rwkv_k_a)
    a_h = heads(a, RWKV_HEADS)
    r_h, k_h, v_h = heads(r, RWKV_HEADS), heads(k, RWKV_HEADS), heads(v, RWKV_HEADS)
    y = rwkv7_scan(r_h, heads(decay, RWKV_HEADS), k_h, v_h, -kk, kk * a_h)
    mu = y.mean(-1, keepdims=True)
    var = jnp.square(y - mu).mean(-1, keepdims=True)
    y = (y - mu) * lax.rsqrt(var + RWKV_LNX_EPS) * rwkv_lnx_g.reshape(RWKV_HEADS, RWKV_HEAD) \
        + rwkv_lnx_b.reshape(RWKV_HEADS, RWKV_HEAD)
    bonus = jnp.sum((r_h * k_h * rwkv_r_k).astype(F32), -1, keepdims=True) * v_h.astype(F32)
    rwkv_out = ((y + bonus).reshape(B, T, RWKV_W) * g.astype(F32)).astype(x.dtype)
    return jnp.concatenate([gla_out, rwkv_out], -1) @ w_out


def dsa_mixer(x, w_in, w_out):
    B, T, _ = x.shape
    q, k, v, qi, ki, wi = _split(x @ w_in, ODD_SPLITS)
    q = q.reshape(B, T, ATT_HEADS, ATT_HEAD)
    qi = qi.reshape(B, T, IDX_HEADS, IDX_DIM)
    cos_a, sin_a = rope_tables(T, ATT_HEAD)
    cos_i, sin_i = rope_tables(T, IDX_DIM)
    q = apply_rope(q, cos_a[:, None], sin_a[:, None])
    k = apply_rope(k, cos_a, sin_a)
    qi = apply_rope(qi, cos_i[:, None], sin_i[:, None])
    ki = apply_rope(ki, cos_i, sin_i)
    wi = wi.astype(F32) * (IDX_HEADS ** -0.5 * IDX_DIM ** -0.5)
    n_sel = min(TOPK_MAX, T // 4)
    key_pos = jnp.arange(T)
    gather = jax.vmap(lambda tab, ids: tab[ids])

    def block(start):
        qb = lax.dynamic_slice_in_dim(q, start, Q_BLOCK, axis=1)
        qib = lax.dynamic_slice_in_dim(qi, start, Q_BLOCK, axis=1)
        wib = lax.dynamic_slice_in_dim(wi, start, Q_BLOCK, axis=1)
        q_pos = start + jnp.arange(Q_BLOCK)
        s = jax.nn.relu(jnp.einsum('bqhd,bsd->bqhs', qib, ki).astype(F32))
        score = jnp.einsum('bqh,bqhs->bqs', wib, s)
        score = jnp.where(key_pos[None, None, :] <= q_pos[None, :, None], score, NEG)
        _, sel = lax.top_k(score, n_sel)
        valid = sel <= q_pos[None, :, None]
        ks, vs = gather(k, sel), gather(v, sel)
        logit = jnp.einsum('bqhd,bqkd->bqhk', qb, ks).astype(F32) * ATT_HEAD ** -0.5
        logit = jnp.where(valid[:, :, None, :], logit, NEG)
        pr = jax.nn.softmax(logit, axis=-1).astype(vs.dtype)
        return jnp.einsum('bqhk,bqkd->bqhd', pr, vs)

    starts = jnp.arange(T // Q_BLOCK, dtype=jnp.int32) * Q_BLOCK
    o = lax.map(block, starts)
    o = jnp.moveaxis(o, 0, 1).reshape(B, T, ATT_HEADS * ATT_HEAD)
    return o @ w_out


def peer_ffn(x, w_q, sub_keys, exp_u, exp_v):
    B, T, D = x.shape
    q = (x @ w_q).reshape(B, T, PEER_HEADS, 2, PEER_DKEY // 2)
    s = jnp.einsum('bthcd,hcnd->bthcn', q, sub_keys).astype(F32)
    top_s, top_i = lax.top_k(s, PEER_TOPK)
    cand_s = (top_s[..., 0, :, None] + top_s[..., 1, None, :]).reshape(B, T, PEER_HEADS, -1)
    cand_i = (top_i[..., 0, :, None] * PEER_NKEYS + top_i[..., 1, None, :]).reshape(B, T, PEER_HEADS, -1)
    best_s, pos = lax.top_k(cand_s, PEER_TOPK)
    idx = jnp.take_along_axis(cand_i, pos, axis=-1)
    gate = jax.nn.softmax(best_s, axis=-1)
    nb = (B * T) // PEER_TOKEN_BLOCK
    hk = PEER_HEADS * PEER_TOPK
    xs = (x.reshape(nb, PEER_TOKEN_BLOCK, D), idx.reshape(nb, PEER_TOKEN_BLOCK, hk),
          gate.reshape(nb, PEER_TOKEN_BLOCK, hk))

    def block(args):
        xb, ib, gb = args
        h = jax.nn.gelu(jnp.einsum('md,mkd->mk', xb, exp_u[ib]).astype(F32), approximate=False)
        coef = (gb * h).astype(xb.dtype)
        return jnp.einsum('mk,mkd->md', coef, exp_v[ib])

    return lax.map(block, xs).reshape(B, T, D)


def setup_inputs(seed: int = 0) -> dict:
    key = jax.random.key(seed)
    ks = iter(jax.random.split(key, 32))

    def nrm(shape, scale):
        return jax.random.normal(next(ks), shape, F32) * scale

    E, O, L, D = N_EVEN, N_ODD, DEPTH, D_MODEL
    return {
        'x': nrm((BATCH, SEQ, D), 1.0),
        'even_w_in': nrm((E, D, EVEN_IN), D ** -0.5),
        'gla_a_w2': nrm((E, GLA_LORA, GLA_KW), GLA_LORA ** -0.5),
        'gla_a_b': nrm((E, GLA_KW), 0.1),
        'gla_norm_g': 1.0 + nrm((E, GLA_DV), 0.05),
        'rwkv_mu': jax.random.uniform(next(ks), (E, RWKV_IN), F32, 0.0, 1.0),
        'rwkv_w0': jax.random.uniform(next(ks), (E, RWKV_W), F32, -6.5, -1.5),
        'rwkv_w2': nrm((E, RWKV_W_LORA, RWKV_W), 0.1 * RWKV_W_LORA ** -0.5),
        'rwkv_a0': nrm((E, RWKV_W), 0.1),
        'rwkv_a2': nrm((E, RWKV_A_LORA, RWKV_W), 0.5 * RWKV_A_LORA ** -0.5),
        'rwkv_g2': nrm((E, RWKV_G_LORA, RWKV_W), 2.0 * RWKV_G_LORA ** -0.5),
        'rwkv_k_k': 0.85 + nrm((E, RWKV_W), 0.05),
        'rwkv_k_a': 1.0 + nrm((E, RWKV_W), 0.05),
        'rwkv_r_k': nrm((E, RWKV_HEADS, RWKV_HEAD), 0.1),
        'rwkv_lnx_g': 1.0 + nrm((E, RWKV_W), 0.05),
        'rwkv_lnx_b': nrm((E, RWKV_W), 0.02),
        'even_w_out': nrm((E, D, D), DN_BETA * D ** -0.5),
        'odd_w_in': nrm((O, D, ODD_IN), D ** -0.5),
        'odd_w_out': nrm((O, D, D), DN_BETA * D ** -0.5),
        'mix_ln_g': 1.0 + nrm((L, D), 0.05),
        'mix_ln_b': nrm((L, D), 0.02),
        'peer_w_q': nrm((L, D, PEER_HEADS * PEER_DKEY), D ** -0.5),
        'peer_sub_keys': nrm((L, PEER_HEADS, 2, PEER_NKEYS, PEER_DKEY // 2), (PEER_DKEY // 2) ** -0.5),
        'peer_u': nrm((L, PEER_N_EXPERTS, D), D ** -0.5),
        'peer_v': nrm((L, PEER_N_EXPERTS, D), DN_BETA * PEER_HEADS ** -0.5),
        'ffn_ln_g': 1.0 + nrm((L, D), 0.05),
        'ffn_ln_b': nrm((L, D), 0.02),
    }


def reference(x, even_w_in, gla_a_w2, gla_a_b, gla_norm_g, rwkv_mu, rwkv_w0, rwkv_w2, rwkv_a0, rwkv_a2,
              rwkv_g2, rwkv_k_k, rwkv_k_a, rwkv_r_k, rwkv_lnx_g, rwkv_lnx_b, even_w_out, odd_w_in,
              odd_w_out, mix_ln_g, mix_ln_b, peer_w_q, peer_sub_keys, peer_u, peer_v, ffn_ln_g, ffn_ln_b):
    for layer in range(DEPTH):
        i = layer // 2
        if layer % 2 == 0:
            mix = even_mixer(x, even_w_in[i], gla_a_w2[i], gla_a_b[i], gla_norm_g[i], rwkv_mu[i], rwkv_w0[i],
                             rwkv_w2[i], rwkv_a0[i], rwkv_a2[i], rwkv_g2[i], rwkv_k_k[i], rwkv_k_a[i],
                             rwkv_r_k[i], rwkv_lnx_g[i], rwkv_lnx_b[i], even_w_out[i])
        else:
            mix = dsa_mixer(x, odd_w_in[i], odd_w_out[i])
        x = layer_norm(DN_ALPHA * x + mix, mix_ln_g[layer], mix_ln_b[layer])
        ffn = peer_ffn(x, peer_w_q[layer], peer_sub_keys[layer], peer_u[layer], peer_v[layer])
        x = layer_norm(DN_ALPHA * x + ffn, ffn_ln_g[layer], ffn_ln_b[layer])
    return x
```

```python
import functools

import jax
import jax.numpy as jnp
import numpy as np
from jax import lax
from jax.experimental import pallas as pl
from jax.experimental.pallas import tpu as pltpu

D_MODEL = 1024
BATCH = 32
SEQ = 2048
DEPTH = 4

GLA_HEADS = 4
GLA_VW = D_MODEL // 2
GLA_DV = GLA_VW // GLA_HEADS
GLA_KW = GLA_VW // 2
GLA_DK = GLA_KW // GLA_HEADS
GLA_LORA = 16
GLA_TAU = 16.0
GLA_CHUNK = 64
RWKV_W = D_MODEL - GLA_VW
RWKV_HEAD = 64
RWKV_HEADS = RWKV_W // RWKV_HEAD
RWKV_W_LORA = 32
RWKV_A_LORA = 32
RWKV_G_LORA = 96
RWKV_LNX_EPS = 64e-5
ATT_HEADS = 8
ATT_HEAD = D_MODEL // ATT_HEADS
IDX_HEADS = 8
IDX_DIM = 64
TOPK_MAX = 256
Q_BLOCK = 128
ROPE_THETA = 10000.0
NEG = -1e30
PEER_HEADS = 8
PEER_DKEY = 128
PEER_NKEYS = 128
PEER_N_EXPERTS = PEER_NKEYS * PEER_NKEYS
PEER_TOPK = 16
PEER_TOKEN_BLOCK = 128
DN_ALPHA = (2 * DEPTH) ** 0.25
DN_BETA = (8 * DEPTH) ** -0.25
LN_EPS = 1e-5
N_EVEN = (DEPTH + 1) // 2
N_ODD = DEPTH // 2

GLA_SPLITS = (GLA_KW, GLA_KW, GLA_VW, GLA_VW, GLA_LORA)
RWKV_SPLITS = (RWKV_W, RWKV_W, RWKV_W, RWKV_W_LORA, RWKV_A_LORA, RWKV_G_LORA)
GLA_IN = sum(GLA_SPLITS)
RWKV_IN = sum(RWKV_SPLITS)
EVEN_IN = GLA_IN + RWKV_IN
ODD_SPLITS = (ATT_HEADS * ATT_HEAD, ATT_HEAD, ATT_HEAD, IDX_HEADS * IDX_DIM, IDX_DIM, IDX_HEADS)
ODD_IN = sum(ODD_SPLITS)

F32 = jnp.float32


def _split(t, sizes):
    cuts = [int(c) for c in np.cumsum(sizes)[:-1]]
    return jnp.split(t, cuts, axis=-1)


LN_ROWS = 512


def _add_ln_body(x_ref, y_ref, g_ref, b_ref, o_ref):
    z = DN_ALPHA * x_ref[...] + y_ref[...]
    mu = jnp.mean(z, axis=-1, keepdims=True)
    zc = z - mu
    var = jnp.mean(zc * zc, axis=-1, keepdims=True)
    o_ref[...] = zc * lax.rsqrt(var + LN_EPS) * g_ref[...] + b_ref[...]


def add_layer_norm(x, y, g, b):
    B, T, D = x.shape
    n = B * T
    row = pl.BlockSpec((LN_ROWS, D), lambda i: (i, 0))
    vec = pl.BlockSpec((1, D), lambda i: (0, 0))
    out = pl.pallas_call(
        _add_ln_body,
        grid=(n // LN_ROWS,),
        in_specs=[row, row, vec, vec],
        out_specs=row,
        out_shape=jax.ShapeDtypeStruct((n, D), F32),
        name="add_layer_norm",
    )(x.reshape(n, D), y.reshape(n, D), g.reshape(1, D), b.reshape(1, D))
    return out.reshape(B, T, D)


def rope_tables(T, dim):
    inv = ROPE_THETA ** (-jnp.arange(0, dim, 2, dtype=F32) / dim)
    ang = jnp.arange(T, dtype=F32)[:, None] * inv[None, :]
    return jnp.cos(ang), jnp.sin(ang)


def apply_rope(x, cos, sin):
    x1, x2 = jnp.split(x, 2, axis=-1)
    return jnp.concatenate([x1 * cos - x2 * sin, x2 * cos + x1 * sin], -1).astype(x.dtype)


def gla_chunked(q, k, v, log_a):
    B, T, H, dk = q.shape
    dv = v.shape[-1]
    C = GLA_CHUNK
    n = T // C

    def chunks(t):
        return t.astype(F32).reshape(B, n, C, H, -1).transpose(0, 3, 1, 2, 4)

    q, k, v, la = chunks(q) * dk ** -0.5, chunks(k), chunks(v), chunks(log_a)
    b = jnp.cumsum(la, axis=3)
    b_last = b[:, :, :, -1:, :]
    q_in = q * jnp.exp(b)
    k_in = k * jnp.exp(-b)
    k_out = k * jnp.exp(b_last - b)
    causal = jnp.tril(jnp.ones((C, C), dtype=bool))
    A = jnp.where(causal, jnp.einsum('bhnid,bhnjd->bhnij', q_in, k_in), 0.0)
    o = jnp.einsum('bhnij,bhnje->bhnie', A, v)
    kv = jnp.einsum('bhnjd,bhnje->bhnde', k_out, v)
    dec = jnp.exp(b_last[:, :, :, 0, :])

    def step(S, inp):
        dec_n, kv_n = inp
        return dec_n[..., None] * S + kv_n, S

    S0 = jnp.zeros((B, H, dk, dv), F32)
    _, S_prev = lax.scan(step, S0, (jnp.moveaxis(dec, 2, 0), jnp.moveaxis(kv, 2, 0)))
    S_prev = jnp.moveaxis(S_prev, 0, 2)
    o = o + jnp.einsum('bhnid,bhnde->bhnie', q_in, S_prev)
    return o.transpose(0, 2, 3, 1, 4).reshape(B, T, H, dv)


def rwkv7_scan(r, w, k, v, a, b):
    B, T, H, N = r.shape

    def step(S, inp):
        r_t, w_t, k_t, v_t, a_t, b_t = inp
        sa = jnp.einsum('bhij,bhj->bhi', S, a_t)
        S = S * w_t[:, :, None, :] + sa[..., None] * b_t[:, :, None, :] + v_t[..., None] * k_t[:, :, None, :]
        return S, jnp.einsum('bhij,bhj->bhi', S, r_t)

    xs = tuple(jnp.moveaxis(t.astype(F32), 1, 0) for t in (r, w, k, v, a, b))
    _, y = lax.scan(step, jnp.zeros((B, H, N, N), F32), xs)
    return jnp.moveaxis(y, 0, 1)


def even_mixer(x, w_in, gla_a_w2, gla_a_b, gla_norm_g, rwkv_mu, rwkv_w0, rwkv_w2, rwkv_a0, rwkv_a2,
               rwkv_g2, rwkv_k_k, rwkv_k_a, rwkv_r_k, rwkv_lnx_g, rwkv_lnx_b, w_out):
    B, T, _ = x.shape

    def heads(t, h):
        return t.reshape(B, T, h, -1)

    p = x @ w_in
    gla_p, rw_p = p[..., :GLA_IN], p[..., GLA_IN:]
    gq, gk, gv, gg, gal = _split(gla_p, GLA_SPLITS)
    log_a = jax.nn.log_sigmoid((gal @ gla_a_w2 + gla_a_b).astype(F32)) / GLA_TAU
    o = gla_chunked(heads(gq, GLA_HEADS), heads(gk, GLA_HEADS), heads(gv, GLA_HEADS), heads(log_a, GLA_HEADS))
    o = o * lax.rsqrt(jnp.mean(jnp.square(o), -1, keepdims=True) + LN_EPS) * gla_norm_g
    gla_out = (o.reshape(B, T, GLA_VW) * jax.nn.silu(gg.astype(F32))).astype(x.dtype)
    prev = jnp.pad(rw_p, ((0, 0), (1, 0), (0, 0)))[:, :-1]
    rw_p = rw_p + (prev - rw_p) * rwkv_mu
    r, k, v, wl, al, gl = _split(rw_p, RWKV_SPLITS)
    w_log = -jax.nn.softplus(-(rwkv_w0 + jnp.tanh(wl) @ rwkv_w2).astype(F32)) - 0.5
    decay = jnp.exp(-jnp.exp(w_log))
    a = jax.nn.sigmoid((rwkv_a0 + al @ rwkv_a2).astype(F32))
    g = jax.nn.sigmoid(gl) @ rwkv_g2
    kk = heads((k * rwkv_k_k).astype(F32), RWKV_HEADS)
    kk = kk / jnp.maximum(jnp.sqrt(jnp.sum(jnp.square(kk), -1, keepdims=True)), 1e-12)
    k = k * (1.0 + (a - 1.0) * rwkv_k_a)
    a_h = heads(a, RWKV_HEADS)
    r_h, k_h, v_h = heads(r, RWKV_HEADS), heads(k, RWKV_HEADS), heads(v, RWKV_HEADS)
    y = rwkv7_scan(r_h, heads(decay, RWKV_HEADS), k_h, v_h, -kk, kk * a_h)
    mu = y.mean(-1, keepdims=True)
    var = jnp.square(y - mu).mean(-1, keepdims=True)
    y = (y - mu) * lax.rsqrt(var + RWKV_LNX_EPS) * rwkv_lnx_g.reshape(RWKV_HEADS, RWKV_HEAD) \
        + rwkv_lnx_b.reshape(RWKV_HEADS, RWKV_HEAD)
    bonus = jnp.sum((r_h * k_h * rwkv_r_k).astype(F32), -1, keepdims=True) * v_h.astype(F32)
    rwkv_out = ((y + bonus).reshape(B, T, RWKV_W) * g.astype(F32)).astype(x.dtype)
    return jnp.concatenate([gla_out, rwkv_out], -1) @ w_out


def dsa_mixer(x, w_in, w_out):
    B, T, _ = x.shape
    q, k, v, qi, ki, wi = _split(x @ w_in, ODD_SPLITS)
    q = q.reshape(B, T, ATT_HEADS, ATT_HEAD)
    qi = qi.reshape(B, T, IDX_HEADS, IDX_DIM)
    cos_a, sin_a = rope_tables(T, ATT_HEAD)
    cos_i, sin_i = rope_tables(T, IDX_DIM)
    q = apply_rope(q, cos_a[:, None], sin_a[:, None])
    k = apply_rope(k, cos_a, sin_a)
    qi = apply_rope(qi, cos_i[:, None], sin_i[:, None])
    ki = apply_rope(ki, cos_i, sin_i)
    wi = wi.astype(F32) * (IDX_HEADS ** -0.5 * IDX_DIM ** -0.5)
    n_sel = min(TOPK_MAX, T // 4)
    key_pos = jnp.arange(T)
    gather = jax.vmap(lambda tab, ids: tab[ids])

    def block(start):
        qb = lax.dynamic_slice_in_dim(q, start, Q_BLOCK, axis=1)
        qib = lax.dynamic_slice_in_dim(qi, start, Q_BLOCK, axis=1)
        wib = lax.dynamic_slice_in_dim(wi, start, Q_BLOCK, axis=1)
        q_pos = start + jnp.arange(Q_BLOCK)
        s = jax.nn.relu(jnp.einsum('bqhd,bsd->bqhs', qib, ki).astype(F32))
        score = jnp.einsum('bqh,bqhs->bqs', wib, s)
        score = jnp.where(key_pos[None, None, :] <= q_pos[None, :, None], score, NEG)
        _, sel = lax.top_k(score, n_sel)
        valid = sel <= q_pos[None, :, None]
        ks, vs = gather(k, sel), gather(v, sel)
        logit = jnp.einsum('bqhd,bqkd->bqhk', qb, ks).astype(F32) * ATT_HEAD ** -0.5
        logit = jnp.where(valid[:, :, None, :], logit, NEG)
        pr = jax.nn.softmax(logit, axis=-1).astype(vs.dtype)
        return jnp.einsum('bqhk,bqkd->bqhd', pr, vs)

    starts = jnp.arange(T // Q_BLOCK, dtype=jnp.int32) * Q_BLOCK
    o = lax.map(block, starts)
    o = jnp.moveaxis(o, 0, 1).reshape(B, T, ATT_HEADS * ATT_HEAD)
    return o @ w_out


def peer_ffn(x, w_q, sub_keys, exp_u, exp_v):
    B, T, D = x.shape
    q = (x @ w_q).reshape(B, T, PEER_HEADS, 2, PEER_DKEY // 2)
    s = jnp.einsum('bthcd,hcnd->bthcn', q, sub_keys).astype(F32)
    top_s, top_i = lax.top_k(s, PEER_TOPK)
    cand_s = (top_s[..., 0, :, None] + top_s[..., 1, None, :]).reshape(B, T, PEER_HEADS, -1)
    cand_i = (top_i[..., 0, :, None] * PEER_NKEYS + top_i[..., 1, None, :]).reshape(B, T, PEER_HEADS, -1)
    best_s, pos = lax.top_k(cand_s, PEER_TOPK)
    idx = jnp.take_along_axis(cand_i, pos, axis=-1)
    gate = jax.nn.softmax(best_s, axis=-1)
    nb = (B * T) // PEER_TOKEN_BLOCK
    hk = PEER_HEADS * PEER_TOPK
    xs = (x.reshape(nb, PEER_TOKEN_BLOCK, D), idx.reshape(nb, PEER_TOKEN_BLOCK, hk),
          gate.reshape(nb, PEER_TOKEN_BLOCK, hk))

    def block(args):
        xb, ib, gb = args
        h = jax.nn.gelu(jnp.einsum('md,mkd->mk', xb, exp_u[ib]).astype(F32), approximate=False)
        coef = (gb * h).astype(xb.dtype)
        return jnp.einsum('mk,mkd->md', coef, exp_v[ib])

    return lax.map(block, xs).reshape(B, T, D)


def kernel(x, even_w_in, gla_a_w2, gla_a_b, gla_norm_g, rwkv_mu, rwkv_w0, rwkv_w2, rwkv_a0, rwkv_a2,
           rwkv_g2, rwkv_k_k, rwkv_k_a, rwkv_r_k, rwkv_lnx_g, rwkv_lnx_b, even_w_out, odd_w_in,
           odd_w_out, mix_ln_g, mix_ln_b, peer_w_q, peer_sub_keys, peer_u, peer_v, ffn_ln_g, ffn_ln_b):
    for layer in range(DEPTH):
        i = layer // 2
        if layer % 2 == 0:
            mix = even_mixer(x, even_w_in[i], gla_a_w2[i], gla_a_b[i], gla_norm_g[i], rwkv_mu[i], rwkv_w0[i],
                             rwkv_w2[i], rwkv_a0[i], rwkv_a2[i], rwkv_g2[i], rwkv_k_k[i], rwkv_k_a[i],
                             rwkv_r_k[i], rwkv_lnx_g[i], rwkv_lnx_b[i], even_w_out[i])
        else:
            mix = dsa_mixer(x, odd_w_in[i], odd_w_out[i])
        x = add_layer_norm(x, mix, mix_ln_g[layer], mix_ln_b[layer])
        ffn = peer_ffn(x, peer_w_q[layer], peer_sub_keys[layer], peer_u[layer], peer_v[layer])
        x = add_layer_norm(x, ffn, ffn_ln_g[layer], ffn_ln_b[layer])
    return x
```

```python
import functools

import jax
import jax.numpy as jnp
import numpy as np
from jax import lax
from jax.experimental import pallas as pl
from jax.experimental.pallas import tpu as pltpu

D_MODEL = 1024
BATCH = 32
SEQ = 2048
DEPTH = 4

GLA_HEADS = 4
GLA_VW = D_MODEL // 2
GLA_DV = GLA_VW // GLA_HEADS
GLA_KW = GLA_VW // 2
GLA_DK = GLA_KW // GLA_HEADS
GLA_LORA = 16
GLA_TAU = 16.0
GLA_CHUNK = 64
RWKV_W = D_MODEL - GLA_VW
RWKV_HEAD = 64
RWKV_HEADS = RWKV_W // RWKV_HEAD
RWKV_W_LORA = 32
RWKV_A_LORA = 32
RWKV_G_LORA = 96
RWKV_LNX_EPS = 64e-5
ATT_HEADS = 8
ATT_HEAD = D_MODEL // ATT_HEADS
IDX_HEADS = 8
IDX_DIM = 64
TOPK_MAX = 256
Q_BLOCK = 128
ROPE_THETA = 10000.0
NEG = -1e30
PEER_HEADS = 8
PEER_DKEY = 128
PEER_NKEYS = 128
PEER_N_EXPERTS = PEER_NKEYS * PEER_NKEYS
PEER_TOPK = 16
PEER_TOKEN_BLOCK = 128
DN_ALPHA = (2 * DEPTH) ** 0.25
DN_BETA = (8 * DEPTH) ** -0.25
LN_EPS = 1e-5
N_EVEN = (DEPTH + 1) // 2
N_ODD = DEPTH // 2

GLA_SPLITS = (GLA_KW, GLA_KW, GLA_VW, GLA_VW, GLA_LORA)
RWKV_SPLITS = (RWKV_W, RWKV_W, RWKV_W, RWKV_W_LORA, RWKV_A_LORA, RWKV_G_LORA)
GLA_IN = sum(GLA_SPLITS)
RWKV_IN = sum(RWKV_SPLITS)
EVEN_IN = GLA_IN + RWKV_IN
ODD_SPLITS = (ATT_HEADS * ATT_HEAD, ATT_HEAD, ATT_HEAD, IDX_HEADS * IDX_DIM, IDX_DIM, IDX_HEADS)
ODD_IN = sum(ODD_SPLITS)

F32 = jnp.float32


def _split(t, sizes):
    cuts = [int(c) for c in np.cumsum(sizes)[:-1]]
    return jnp.split(t, cuts, axis=-1)


LN_ROWS = 512


def _add_ln_body(x_ref, y_ref, g_ref, b_ref, o_ref):
    z = DN_ALPHA * x_ref[...] + y_ref[...]
    mu = jnp.mean(z, axis=-1, keepdims=True)
    zc = z - mu
    var = jnp.mean(zc * zc, axis=-1, keepdims=True)
    o_ref[...] = zc * lax.rsqrt(var + LN_EPS) * g_ref[...] + b_ref[...]


def add_layer_norm(x, y, g, b):
    B, T, D = x.shape
    n = B * T
    row = pl.BlockSpec((LN_ROWS, D), lambda i: (i, 0))
    vec = pl.BlockSpec((1, D), lambda i: (0, 0))
    out = pl.pallas_call(
        _add_ln_body,
        grid=(n // LN_ROWS,),
        in_specs=[row, row, vec, vec],
        out_specs=row,
        out_shape=jax.ShapeDtypeStruct((n, D), F32),
        name="add_layer_norm",
    )(x.reshape(n, D), y.reshape(n, D), g.reshape(1, D), b.reshape(1, D))
    return out.reshape(B, T, D)


def rope_tables(T, dim):
    inv = ROPE_THETA ** (-jnp.arange(0, dim, 2, dtype=F32) / dim)
    ang = jnp.arange(T, dtype=F32)[:, None] * inv[None, :]
    return jnp.cos(ang), jnp.sin(ang)


def apply_rope(x, cos, sin):
    x1, x2 = jnp.split(x, 2, axis=-1)
    return jnp.concatenate([x1 * cos - x2 * sin, x2 * cos + x1 * sin], -1).astype(x.dtype)


def gla_chunked(q, k, v, log_a):
    B, T, H, dk = q.shape
    dv = v.shape[-1]
    C = GLA_CHUNK
    n = T // C

    def chunks(t):
        return t.astype(F32).reshape(B, n, C, H, -1).transpose(0, 3, 1, 2, 4)

    q, k, v, la = chunks(q) * dk ** -0.5, chunks(k), chunks(v), chunks(log_a)
    b = jnp.cumsum(la, axis=3)
    b_last = b[:, :, :, -1:, :]
    q_in = q * jnp.exp(b)
    k_in = k * jnp.exp(-b)
    k_out = k * jnp.exp(b_last - b)
    causal = jnp.tril(jnp.ones((C, C), dtype=bool))
    A = jnp.where(causal, jnp.einsum('bhnid,bhnjd->bhnij', q_in, k_in), 0.0)
    o = jnp.einsum('bhnij,bhnje->bhnie', A, v)
    kv = jnp.einsum('bhnjd,bhnje->bhnde', k_out, v)
    dec = jnp.exp(b_last[:, :, :, 0, :])

    def step(S, inp):
        dec_n, kv_n = inp
        return dec_n[..., None] * S + kv_n, S

    S0 = jnp.zeros((B, H, dk, dv), F32)
    _, S_prev = lax.scan(step, S0, (jnp.moveaxis(dec, 2, 0), jnp.moveaxis(kv, 2, 0)))
    S_prev = jnp.moveaxis(S_prev, 0, 2)
    o = o + jnp.einsum('bhnid,bhnde->bhnie', q_in, S_prev)
    return o.transpose(0, 2, 3, 1, 4).reshape(B, T, H, dv)


def rwkv7_scan(r, w, k, v, a, b):
    B, T, H, N = r.shape

    def step(S, inp):
        r_t, w_t, k_t, v_t, a_t, b_t = inp
        sa = jnp.einsum('bhij,bhj->bhi', S, a_t)
        S = S * w_t[:, :, None, :] + sa[..., None] * b_t[:, :, None, :] + v_t[..., None] * k_t[:, :, None, :]
        return S, jnp.einsum('bhij,bhj->bhi', S, r_t)

    xs = tuple(jnp.moveaxis(t.astype(F32), 1, 0) for t in (r, w, k, v, a, b))
    _, y = lax.scan(step, jnp.zeros((B, H, N, N), F32), xs)
    return jnp.moveaxis(y, 0, 1)


def even_mixer(x, w_in, gla_a_w2, gla_a_b, gla_norm_g, rwkv_mu, rwkv_w0, rwkv_w2, rwkv_a0, rwkv_a2,
               rwkv_g2, rwkv_k_k, rwkv_k_a, rwkv_r_k, rwkv_lnx_g, rwkv_lnx_b, w_out):
    B, T, _ = x.shape

    def heads(t, h):
        return t.reshape(B, T, h, -1)

    p = x @ w_in
    gla_p, rw_p = p[..., :GLA_IN], p[..., GLA_IN:]
    gq, gk, gv, gg, gal = _split(gla_p, GLA_SPLITS)
    log_a = jax.nn.log_sigmoid((gal @ gla_a_w2 + gla_a_b).astype(F32)) / GLA_TAU
    o = gla_chunked(heads(gq, GLA_HEADS), heads(gk, GLA_HEADS), heads(gv, GLA_HEADS), heads(log_a, GLA_HEADS))
    o = o * lax.rsqrt(jnp.mean(jnp.square(o), -1, keepdims=True) + LN_EPS) * gla_norm_g
    gla_out = (o.reshape(B, T, GLA_VW) * jax.nn.silu(gg.astype(F32))).astype(x.dtype)
    prev = jnp.pad(rw_p, ((0, 0), (1, 0), (0, 0)))[:, :-1]
    rw_p = rw_p + (prev - rw_p) * rwkv_mu
    r, k, v, wl, al, gl = _split(rw_p, RWKV_SPLITS)
    w_log = -jax.nn.softplus(-(rwkv_w0 + jnp.tanh(wl) @ rwkv_w2).astype(F32)) - 0.5
    decay = jnp.exp(-jnp.exp(w_log))
    a = jax.nn.sigmoid((rwkv_a0 + al @ rwkv_a2).astype(F32))
    g = jax.nn.sigmoid(gl) @ rwkv_g2
    kk = heads((k * rwkv_k_k).astype(F32), RWKV_HEADS)
    kk = kk / jnp.maximum(jnp.sqrt(jnp.sum(jnp.square(kk), -1, keepdims=True)), 1e-12)
    k = k * (1.0 + (a - 1.0) * rwkv_k_a)
    a_h = heads(a, RWKV_HEADS)
    r_h, k_h, v_h = heads(r, RWKV_HEADS), heads(k, RWKV_HEADS), heads(v, RWKV_HEADS)
    y = rwkv7_scan(r_h, heads(decay, RWKV_HEADS), k_h, v_h, -kk, kk * a_h)
    mu = y.mean(-1, keepdims=True)
    var = jnp.square(y - mu).mean(-1, keepdims=True)
    y = (y - mu) * lax.rsqrt(var + RWKV_LNX_EPS) * rwkv_lnx_g.reshape(RWKV_HEADS, RWKV_HEAD) \
        + rwkv_lnx_b.reshape(RWKV_HEADS, RWKV_HEAD)
    bonus = jnp.sum((r_h * k_h * rwkv_r_k).astype(F32), -1, keepdims=True) * v_h.astype(F32)
    rwkv_out = ((y + bonus).reshape(B, T, RWKV_W) * g.astype(F32)).astype(x.dtype)
    return jnp.concatenate([gla_out, rwkv_out], -1) @ w_out


MM_ROWS = 512
VMEM_LIMIT = 48 * 1024 * 1024


def _mm_body(x_ref, w_ref, o_ref):
    o_ref[...] = jnp.dot(x_ref[...].astype(jnp.bfloat16), w_ref[...], preferred_element_type=F32)


def matmul(x, w):
    M, K = x.shape
    N = w.shape[1]
    tm = min(MM_ROWS, M)
    return pl.pallas_call(
        _mm_body,
        grid=(M // tm,),
        in_specs=[pl.BlockSpec((tm, K), lambda i: (i, 0)), pl.BlockSpec((K, N), lambda i: (0, 0))],
        out_specs=pl.BlockSpec((tm, N), lambda i: (i, 0)),
        out_shape=jax.ShapeDtypeStruct((M, N), F32),
        compiler_params=pltpu.CompilerParams(vmem_limit_bytes=VMEM_LIMIT),
        name="matmul",
    )(x, w.astype(jnp.bfloat16))


LANES = 128
INT_MIN = -2 ** 31
DSA_Q0 = 0
DSA_QI0 = ATT_HEADS * ATT_HEAD
DSA_K0 = DSA_QI0 + IDX_HEADS * IDX_DIM
DSA_V0 = DSA_K0 + ATT_HEAD
DSA_KIW0 = DSA_V0 + ATT_HEAD
DSA_COLS = DSA_KIW0 + LANES


def _swap32(x):
    lane = lax.broadcasted_iota(jnp.int32, x.shape, 1)
    return jnp.where((lane % IDX_DIM) < IDX_DIM // 2, pltpu.roll(x, LANES - IDX_DIM // 2, 1),
                     pltpu.roll(x, IDX_DIM // 2, 1))


def _count_ge(key, cand):
    return jnp.sum(jnp.where(key >= cand, 1.0, 0.0), axis=1, keepdims=True)


def _dsa_body(n_sel, q_ref, qi_ref, kiwq_ref, k_ref, v_ref, kiw_ref, cqb_ref, sqb_ref, cib_ref, sib_ref,
              cq_ref, sq_ref, ci_ref, si_ref, o_ref, krot_ref, vbf_ref, kirot_ref):
    j = pl.program_id(1)
    T = k_ref.shape[0]
    QB = q_ref.shape[0]

    @pl.when(j == 0)
    def _():
        k = k_ref[...]
        krot_ref[...] = (k * cq_ref[...] + pltpu.roll(k, ATT_HEAD // 2, 1) * sq_ref[...]).astype(jnp.bfloat16)
        vbf_ref[...] = v_ref[...].astype(jnp.bfloat16)
        kiw = kiw_ref[...]
        kirot_ref[...] = (kiw * ci_ref[...] + _swap32(kiw) * si_ref[...]).astype(jnp.bfloat16)

    ci_b = cib_ref[...]
    si_b = sib_ref[...]
    ki = kirot_ref[:, :IDX_DIM]
    wi = kiwq_ref[:, IDX_DIM:IDX_DIM + IDX_HEADS] * (IDX_HEADS ** -0.5 * IDX_DIM ** -0.5)
    score = jnp.zeros((QB, T), F32)
    for g in range(IDX_HEADS * IDX_DIM // LANES):
        qg = qi_ref[:, g * LANES:(g + 1) * LANES]
        qg = (qg * ci_b + _swap32(qg) * si_b).astype(jnp.bfloat16)
        for hh in range(LANES // IDX_DIM):
            h = g * (LANES // IDX_DIM) + hh
            s = lax.dot_general(qg[:, hh * IDX_DIM:(hh + 1) * IDX_DIM], ki, (((1,), (1,)), ((), ())),
                                preferred_element_type=F32)
            score = score + wi[:, h:h + 1] * jnp.maximum(s, 0.0)
    col = lax.broadcasted_iota(jnp.int32, (QB, T), 1)
    row = j * QB + lax.broadcasted_iota(jnp.int32, (QB, T), 0)
    causal = col <= row
    score = jnp.where(causal, score, NEG)

    bits = pltpu.bitcast(score, jnp.int32)
    key = jnp.where(bits < 0, bits ^ jnp.int32(0x7FFFFFFF), bits)
    kf = float(n_sel)
    t0 = jnp.where(_count_ge(key, jnp.zeros((QB, 1), jnp.int32)) >= kf, jnp.int32(0), jnp.int32(INT_MIN))

    def bit_step(i, t):
        cand = t | (jnp.int32(1) << (30 - i))
        return jnp.where(_count_ge(key, cand) >= kf, cand, t)

    thr = lax.fori_loop(0, 31, bit_step, t0)
    gt = key > thr
    eq = key == thr
    need = kf - jnp.sum(jnp.where(gt, 1.0, 0.0), axis=1, keepdims=True)

    def tie_step(i, jj):
        cand = jj | (jnp.int32(1) << (T.bit_length() - 2 - i))
        c = jnp.sum(jnp.where(eq & (col < cand), 1.0, 0.0), axis=1, keepdims=True)
        return jnp.where(c < need, cand, jj)

    jstar = lax.fori_loop(0, T.bit_length() - 1, tie_step, jnp.zeros((QB, 1), jnp.int32))
    sel = (gt | (eq & (col <= jstar))) & causal
    bias = jnp.where(sel, 0.0, NEG)

    cq_b = cqb_ref[...]
    sq_b = sqb_ref[...]
    kr = krot_ref[...]
    vb = vbf_ref[...]
    for h in range(ATT_HEADS):
        qh = q_ref[:, h * ATT_HEAD:(h + 1) * ATT_HEAD]
        qh = (qh * cq_b + pltpu.roll(qh, ATT_HEAD // 2, 1) * sq_b).astype(jnp.bfloat16)
        logit = lax.dot_general(qh, kr, (((1,), (1,)), ((), ())), preferred_element_type=F32)
        logit = logit * (ATT_HEAD ** -0.5) + bias
        m = jnp.max(logit, axis=1, keepdims=True)
        p = jnp.exp(logit - m)
        l = jnp.sum(p, axis=1, keepdims=True)
        o = jnp.dot(p.astype(jnp.bfloat16), vb, preferred_element_type=F32)
        o_ref[:, h * ATT_HEAD:(h + 1) * ATT_HEAD] = o / l


def dsa_attention(p, B, T):
    n_sel = min(TOPK_MAX, T // 4)
    nq = T // Q_BLOCK
    cos_a, sin_a = rope_tables(T, ATT_HEAD)
    cos_i, sin_i = rope_tables(T, IDX_DIM)
    cq = jnp.concatenate([cos_a, cos_a], -1)
    sq = jnp.concatenate([-sin_a, sin_a], -1)
    ci = jnp.concatenate([cos_i, cos_i, cos_i, cos_i], -1)
    si = jnp.concatenate([-sin_i, sin_i, -sin_i, sin_i], -1)

    def qrow(width, col0):
        return pl.BlockSpec((Q_BLOCK, width), lambda b, j: (b * nq + j, col0 // width))

    def krow(col0):
        return pl.BlockSpec((T, LANES), lambda b, j: (b, col0 // LANES))

    tab_q = pl.BlockSpec((Q_BLOCK, LANES), lambda b, j: (j, 0))
    tab_k = pl.BlockSpec((T, LANES), lambda b, j: (0, 0))
    return pl.pallas_call(
        functools.partial(_dsa_body, n_sel),
        grid=(B, nq),
        in_specs=[qrow(ATT_HEADS * ATT_HEAD, DSA_Q0), qrow(IDX_HEADS * IDX_DIM, DSA_QI0), qrow(LANES, DSA_KIW0),
                  krow(DSA_K0), krow(DSA_V0), krow(DSA_KIW0),
                  tab_q, tab_q, tab_q, tab_q, tab_k, tab_k, tab_k, tab_k],
        out_specs=pl.BlockSpec((Q_BLOCK, D_MODEL), lambda b, j: (b * nq + j, 0)),
        out_shape=jax.ShapeDtypeStruct((B * T, D_MODEL), F32),
        scratch_shapes=[pltpu.VMEM((T, ATT_HEAD), jnp.bfloat16), pltpu.VMEM((T, ATT_HEAD), jnp.bfloat16),
                        pltpu.VMEM((T, LANES), jnp.bfloat16)],
        compiler_params=pltpu.CompilerParams(dimension_semantics=("arbitrary", "arbitrary"),
                                             vmem_limit_bytes=VMEM_LIMIT),
        name="dsa_attention",
    )(p, p, p, p, p, p, cq, sq, ci, si, cq, sq, ci, si)


def dsa_mixer(x, w_in, w_out):
    B, T, D = x.shape
    wq, wk, wv, wqi, wki, wwi = _split(w_in, ODD_SPLITS)
    pad = jnp.zeros((D, DSA_COLS - DSA_KIW0 - IDX_DIM - IDX_HEADS), w_in.dtype)
    w_cat = jnp.concatenate([wq, wqi, wk, wv, wki, wwi, pad], axis=1)
    p = matmul(x.reshape(B * T, D), w_cat)
    o = dsa_attention(p, B, T)
    return matmul(o, w_out).reshape(B, T, D)


def peer_ffn(x, w_q, sub_keys, exp_u, exp_v):
    B, T, D = x.shape
    q = (x @ w_q).reshape(B, T, PEER_HEADS, 2, PEER_DKEY // 2)
    s = jnp.einsum('bthcd,hcnd->bthcn', q, sub_keys).astype(F32)
    top_s, top_i = lax.top_k(s, PEER_TOPK)
    cand_s = (top_s[..., 0, :, None] + top_s[..., 1, None, :]).reshape(B, T, PEER_HEADS, -1)
    cand_i = (top_i[..., 0, :, None] * PEER_NKEYS + top_i[..., 1, None, :]).reshape(B, T, PEER_HEADS, -1)
    best_s, pos = lax.top_k(cand_s, PEER_TOPK)
    idx = jnp.take_along_axis(cand_i, pos, axis=-1)
    gate = jax.nn.softmax(best_s, axis=-1)
    nb = (B * T) // PEER_TOKEN_BLOCK
    hk = PEER_HEADS * PEER_TOPK
    xs = (x.reshape(nb, PEER_TOKEN_BLOCK, D), idx.reshape(nb, PEER_TOKEN_BLOCK, hk),
          gate.reshape(nb, PEER_TOKEN_BLOCK, hk))

    def block(args):
        xb, ib, gb = args
        h = jax.nn.gelu(jnp.einsum('md,mkd->mk', xb, exp_u[ib]).astype(F32), approximate=False)
        coef = (gb * h).astype(xb.dtype)
        return jnp.einsum('mk,mkd->md', coef, exp_v[ib])

    return lax.map(block, xs).reshape(B, T, D)


def kernel(x, even_w_in, gla_a_w2, gla_a_b, gla_norm_g, rwkv_mu, rwkv_w0, rwkv_w2, rwkv_a0, rwkv_a2,
           rwkv_g2, rwkv_k_k, rwkv_k_a, rwkv_r_k, rwkv_lnx_g, rwkv_lnx_b, even_w_out, odd_w_in,
           odd_w_out, mix_ln_g, mix_ln_b, peer_w_q, peer_sub_keys, peer_u, peer_v, ffn_ln_g, ffn_ln_b):
    for layer in range(DEPTH):
        i = layer // 2
        if layer % 2 == 0:
            mix = even_mixer(x, even_w_in[i], gla_a_w2[i], gla_a_b[i], gla_norm_g[i], rwkv_mu[i], rwkv_w0[i],
                             rwkv_w2[i], rwkv_a0[i], rwkv_a2[i], rwkv_g2[i], rwkv_k_k[i], rwkv_k_a[i],
                             rwkv_r_k[i], rwkv_lnx_g[i], rwkv_lnx_b[i], even_w_out[i])
        else:
            mix = dsa_mixer(x, odd_w_in[i], odd_w_out[i])
        x = add_layer_norm(x, mix, mix_ln_g[layer], mix_ln_b[layer])
        ffn = peer_ffn(x, peer_w_q[layer], peer_sub_keys[layer], peer_u[layer], peer_v[layer])
        x = add_layer_norm(x, ffn, ffn_ln_g[layer], ffn_ln_b[layer])
    return x
```

```python
import functools

import jax
import jax.numpy as jnp
import numpy as np
from jax import lax
from jax.experimental import pallas as pl
from jax.experimental.pallas import tpu as pltpu

D_MODEL = 1024
BATCH = 32
SEQ = 2048
DEPTH = 4

GLA_HEADS = 4
GLA_VW = D_MODEL // 2
GLA_DV = GLA_VW // GLA_HEADS
GLA_KW = GLA_VW // 2
GLA_DK = GLA_KW // GLA_HEADS
GLA_LORA = 16
GLA_TAU = 16.0
GLA_CHUNK = 64
RWKV_W = D_MODEL - GLA_VW
RWKV_HEAD = 64
RWKV_HEADS = RWKV_W // RWKV_HEAD
RWKV_W_LORA = 32
RWKV_A_LORA = 32
RWKV_G_LORA = 96
RWKV_LNX_EPS = 64e-5
ATT_HEADS = 8
ATT_HEAD = D_MODEL // ATT_HEADS
IDX_HEADS = 8
IDX_DIM = 64
TOPK_MAX = 256
Q_BLOCK = 128
ROPE_THETA = 10000.0
NEG = -1e30
PEER_HEADS = 8
PEER_DKEY = 128
PEER_NKEYS = 128
PEER_N_EXPERTS = PEER_NKEYS * PEER_NKEYS
PEER_TOPK = 16
PEER_TOKEN_BLOCK = 128
DN_ALPHA = (2 * DEPTH) ** 0.25
DN_BETA = (8 * DEPTH) ** -0.25
LN_EPS = 1e-5
N_EVEN = (DEPTH + 1) // 2
N_ODD = DEPTH // 2

GLA_SPLITS = (GLA_KW, GLA_KW, GLA_VW, GLA_VW, GLA_LORA)
RWKV_SPLITS = (RWKV_W, RWKV_W, RWKV_W, RWKV_W_LORA, RWKV_A_LORA, RWKV_G_LORA)
GLA_IN = sum(GLA_SPLITS)
RWKV_IN = sum(RWKV_SPLITS)
EVEN_IN = GLA_IN + RWKV_IN
ODD_SPLITS = (ATT_HEADS * ATT_HEAD, ATT_HEAD, ATT_HEAD, IDX_HEADS * IDX_DIM, IDX_DIM, IDX_HEADS)
ODD_IN = sum(ODD_SPLITS)

F32 = jnp.float32


def _split(t, sizes):
    cuts = [int(c) for c in np.cumsum(sizes)[:-1]]
    return jnp.split(t, cuts, axis=-1)


LN_ROWS = 512


def _add_ln_body(x_ref, y_ref, g_ref, b_ref, o_ref):
    z = DN_ALPHA * x_ref[...] + y_ref[...]
    mu = jnp.mean(z, axis=-1, keepdims=True)
    zc = z - mu
    var = jnp.mean(zc * zc, axis=-1, keepdims=True)
    o_ref[...] = zc * lax.rsqrt(var + LN_EPS) * g_ref[...] + b_ref[...]


def add_layer_norm(x, y, g, b):
    B, T, D = x.shape
    n = B * T
    row = pl.BlockSpec((LN_ROWS, D), lambda i: (i, 0))
    vec = pl.BlockSpec((1, D), lambda i: (0, 0))
    out = pl.pallas_call(
        _add_ln_body,
        grid=(n // LN_ROWS,),
        in_specs=[row, row, vec, vec],
        out_specs=row,
        out_shape=jax.ShapeDtypeStruct((n, D), F32),
        name="add_layer_norm",
    )(x.reshape(n, D), y.reshape(n, D), g.reshape(1, D), b.reshape(1, D))
    return out.reshape(B, T, D)


def rope_tables(T, dim):
    inv = ROPE_THETA ** (-jnp.arange(0, dim, 2, dtype=F32) / dim)
    ang = jnp.arange(T, dtype=F32)[:, None] * inv[None, :]
    return jnp.cos(ang), jnp.sin(ang)


def apply_rope(x, cos, sin):
    x1, x2 = jnp.split(x, 2, axis=-1)
    return jnp.concatenate([x1 * cos - x2 * sin, x2 * cos + x1 * sin], -1).astype(x.dtype)


def gla_chunked(q, k, v, log_a):
    B, T, H, dk = q.shape
    dv = v.shape[-1]
    C = GLA_CHUNK
    n = T // C

    def chunks(t):
        return t.astype(F32).reshape(B, n, C, H, -1).transpose(0, 3, 1, 2, 4)

    q, k, v, la = chunks(q) * dk ** -0.5, chunks(k), chunks(v), chunks(log_a)
    b = jnp.cumsum(la, axis=3)
    b_last = b[:, :, :, -1:, :]
    q_in = q * jnp.exp(b)
    k_in = k * jnp.exp(-b)
    k_out = k * jnp.exp(b_last - b)
    causal = jnp.tril(jnp.ones((C, C), dtype=bool))
    A = jnp.where(causal, jnp.einsum('bhnid,bhnjd->bhnij', q_in, k_in), 0.0)
    o = jnp.einsum('bhnij,bhnje->bhnie', A, v)
    kv = jnp.einsum('bhnjd,bhnje->bhnde', k_out, v)
    dec = jnp.exp(b_last[:, :, :, 0, :])

    def step(S, inp):
        dec_n, kv_n = inp
        return dec_n[..., None] * S + kv_n, S

    S0 = jnp.zeros((B, H, dk, dv), F32)
    _, S_prev = lax.scan(step, S0, (jnp.moveaxis(dec, 2, 0), jnp.moveaxis(kv, 2, 0)))
    S_prev = jnp.moveaxis(S_prev, 0, 2)
    o = o + jnp.einsum('bhnid,bhnde->bhnie', q_in, S_prev)
    return o.transpose(0, 2, 3, 1, 4).reshape(B, T, H, dv)


def rwkv7_scan(r, w, k, v, a, b):
    B, T, H, N = r.shape

    def step(S, inp):
        r_t, w_t, k_t, v_t, a_t, b_t = inp
        sa = jnp.einsum('bhij,bhj->bhi', S, a_t)
        S = S * w_t[:, :, None, :] + sa[..., None] * b_t[:, :, None, :] + v_t[..., None] * k_t[:, :, None, :]
        return S, jnp.einsum('bhij,bhj->bhi', S, r_t)

    xs = tuple(jnp.moveaxis(t.astype(F32), 1, 0) for t in (r, w, k, v, a, b))
    _, y = lax.scan(step, jnp.zeros((B, H, N, N), F32), xs)
    return jnp.moveaxis(y, 0, 1)


def even_mixer(x, w_in, gla_a_w2, gla_a_b, gla_norm_g, rwkv_mu, rwkv_w0, rwkv_w2, rwkv_a0, rwkv_a2,
               rwkv_g2, rwkv_k_k, rwkv_k_a, rwkv_r_k, rwkv_lnx_g, rwkv_lnx_b, w_out):
    B, T, _ = x.shape

    def heads(t, h):
        return t.reshape(B, T, h, -1)

    p = x @ w_in
    gla_p, rw_p = p[..., :GLA_IN], p[..., GLA_IN:]
    gq, gk, gv, gg, gal = _split(gla_p, GLA_SPLITS)
    log_a = jax.nn.log_sigmoid((gal @ gla_a_w2 + gla_a_b).astype(F32)) / GLA_TAU
    o = gla_chunked(heads(gq, GLA_HEADS), heads(gk, GLA_HEADS), heads(gv, GLA_HEADS), heads(log_a, GLA_HEADS))
    o = o * lax.rsqrt(jnp.mean(jnp.square(o), -1, keepdims=True) + LN_EPS) * gla_norm_g
    gla_out = (o.reshape(B, T, GLA_VW) * jax.nn.silu(gg.astype(F32))).astype(x.dtype)
    prev = jnp.pad(rw_p, ((0, 0), (1, 0), (0, 0)))[:, :-1]
    rw_p = rw_p + (prev - rw_p) * rwkv_mu
    r, k, v, wl, al, gl = _split(rw_p, RWKV_SPLITS)
    w_log = -jax.nn.softplus(-(rwkv_w0 + jnp.tanh(wl) @ rwkv_w2).astype(F32)) - 0.5
    decay = jnp.exp(-jnp.exp(w_log))
    a = jax.nn.sigmoid((rwkv_a0 + al @ rwkv_a2).astype(F32))
    g = jax.nn.sigmoid(gl) @ rwkv_g2
    kk = heads((k * rwkv_k_k).astype(F32), RWKV_HEADS)
    kk = kk / jnp.maximum(jnp.sqrt(jnp.sum(jnp.square(kk), -1, keepdims=True)), 1e-12)
    k = k * (1.0 + (a - 1.0) * rwkv_k_a)
    a_h = heads(a, RWKV_HEADS)
    r_h, k_h, v_h = heads(r, RWKV_HEADS), heads(k, RWKV_HEADS), heads(v, RWKV_HEADS)
    y = rwkv7_scan(r_h, heads(decay, RWKV_HEADS), k_h, v_h, -kk, kk * a_h)
    mu = y.mean(-1, keepdims=True)
    var = jnp.square(y - mu).mean(-1, keepdims=True)
    y = (y - mu) * lax.rsqrt(var + RWKV_LNX_EPS) * rwkv_lnx_g.reshape(RWKV_HEADS, RWKV_HEAD) \
        + rwkv_lnx_b.reshape(RWKV_HEADS, RWKV_HEAD)
    bonus = jnp.sum((r_h * k_h * rwkv_r_k).astype(F32), -1, keepdims=True) * v_h.astype(F32)
    rwkv_out = ((y + bonus).reshape(B, T, RWKV_W) * g.astype(F32)).astype(x.dtype)
    return jnp.concatenate([gla_out, rwkv_out], -1) @ w_out


MM_ROWS = 512
VMEM_LIMIT = 48 * 1024 * 1024


def _mm_body(x_ref, w_ref, o_ref):
    o_ref[...] = jnp.dot(x_ref[...].astype(jnp.bfloat16), w_ref[...], preferred_element_type=F32)


def matmul(x, w):
    M, K = x.shape
    N = w.shape[1]
    tm = min(MM_ROWS, M)
    return pl.pallas_call(
        _mm_body,
        grid=(M // tm,),
        in_specs=[pl.BlockSpec((tm, K), lambda i: (i, 0)), pl.BlockSpec((K, N), lambda i: (0, 0))],
        out_specs=pl.BlockSpec((tm, N), lambda i: (i, 0)),
        out_shape=jax.ShapeDtypeStruct((M, N), F32),
        compiler_params=pltpu.CompilerParams(vmem_limit_bytes=VMEM_LIMIT),
        name="matmul",
    )(x, w.astype(jnp.bfloat16))


LANES = 128
INT_MIN = -2 ** 31
DSA_Q0 = 0
DSA_QI0 = ATT_HEADS * ATT_HEAD
DSA_K0 = DSA_QI0 + IDX_HEADS * IDX_DIM
DSA_V0 = DSA_K0 + ATT_HEAD
DSA_KIW0 = DSA_V0 + ATT_HEAD
DSA_COLS = DSA_KIW0 + LANES


def _swap32(x):
    lane = lax.broadcasted_iota(jnp.int32, x.shape, 1)
    return jnp.where((lane % IDX_DIM) < IDX_DIM // 2, pltpu.roll(x, LANES - IDX_DIM // 2, 1),
                     pltpu.roll(x, IDX_DIM // 2, 1))


def _count_ge(key, cand):
    return jnp.sum(jnp.where(key >= cand, 1.0, 0.0), axis=1, keepdims=True)


def _dsa_body(n_sel, q_ref, qi_ref, kiwq_ref, k_ref, v_ref, kiw_ref, cqb_ref, sqb_ref, cib_ref, sib_ref,
              cq_ref, sq_ref, ci_ref, si_ref, o_ref, krot_ref, vbf_ref, kirot_ref):
    j = pl.program_id(1)
    T = k_ref.shape[0]
    QB = q_ref.shape[0]

    @pl.when(j == 0)
    def _():
        k = k_ref[...]
        krot_ref[...] = (k * cq_ref[...] + pltpu.roll(k, ATT_HEAD // 2, 1) * sq_ref[...]).astype(jnp.bfloat16)
        vbf_ref[...] = v_ref[...].astype(jnp.bfloat16)
        kiw = kiw_ref[...]
        kirot_ref[...] = (kiw * ci_ref[...] + _swap32(kiw) * si_ref[...]).astype(jnp.bfloat16)

    ci_b = cib_ref[...]
    si_b = sib_ref[...]
    ki = kirot_ref[:, :IDX_DIM]
    wi = kiwq_ref[:, IDX_DIM:IDX_DIM + IDX_HEADS] * (IDX_HEADS ** -0.5 * IDX_DIM ** -0.5)
    score = jnp.zeros((QB, T), F32)
    for g in range(IDX_HEADS * IDX_DIM // LANES):
        qg = qi_ref[:, g * LANES:(g + 1) * LANES]
        qg = (qg * ci_b + _swap32(qg) * si_b).astype(jnp.bfloat16)
        for hh in range(LANES // IDX_DIM):
            h = g * (LANES // IDX_DIM) + hh
            s = lax.dot_general(qg[:, hh * IDX_DIM:(hh + 1) * IDX_DIM], ki, (((1,), (1,)), ((), ())),
                                preferred_element_type=F32)
            score = score + wi[:, h:h + 1] * jnp.maximum(s, 0.0)
    col = lax.broadcasted_iota(jnp.int32, (QB, T), 1)
    row = j * QB + lax.broadcasted_iota(jnp.int32, (QB, T), 0)
    causal = col <= row
    score = jnp.where(causal, score, NEG)

    bits = pltpu.bitcast(score, jnp.int32)
    key = jnp.where(bits < 0, bits ^ jnp.int32(0x7FFFFFFF), bits)
    kf = float(n_sel)
    t0 = jnp.where(_count_ge(key, jnp.zeros((QB, 1), jnp.int32)) >= kf, jnp.int32(0), jnp.int32(INT_MIN))

    def bit_step(i, t):
        cand = t | (jnp.int32(1) << (30 - i))
        return jnp.where(_count_ge(key, cand) >= kf, cand, t)

    thr = lax.fori_loop(0, 31, bit_step, t0)
    gt = key > thr
    eq = key == thr
    need = kf - jnp.sum(jnp.where(gt, 1.0, 0.0), axis=1, keepdims=True)

    def tie_step(i, jj):
        cand = jj | (jnp.int32(1) << (T.bit_length() - 2 - i))
        c = jnp.sum(jnp.where(eq & (col < cand), 1.0, 0.0), axis=1, keepdims=True)
        return jnp.where(c < need, cand, jj)

    jstar = lax.fori_loop(0, T.bit_length() - 1, tie_step, jnp.zeros((QB, 1), jnp.int32))
    sel = (gt | (eq & (col <= jstar))) & causal
    bias = jnp.where(sel, 0.0, NEG)

    cq_b = cqb_ref[...]
    sq_b = sqb_ref[...]
    kr = krot_ref[...]
    vb = vbf_ref[...]
    for h in range(ATT_HEADS):
        qh = q_ref[:, h * ATT_HEAD:(h + 1) * ATT_HEAD]
        qh = (qh * cq_b + pltpu.roll(qh, ATT_HEAD // 2, 1) * sq_b).astype(jnp.bfloat16)
        logit = lax.dot_general(qh, kr, (((1,), (1,)), ((), ())), preferred_element_type=F32)
        logit = logit * (ATT_HEAD ** -0.5) + bias
        m = jnp.max(logit, axis=1, keepdims=True)
        p = jnp.exp(logit - m)
        l = jnp.sum(p, axis=1, keepdims=True)
        o = jnp.dot(p.astype(jnp.bfloat16), vb, preferred_element_type=F32)
        o_ref[:, h * ATT_HEAD:(h + 1) * ATT_HEAD] = o / l


def dsa_attention(p, B, T):
    n_sel = min(TOPK_MAX, T // 4)
    nq = T // Q_BLOCK
    cos_a, sin_a = rope_tables(T, ATT_HEAD)
    cos_i, sin_i = rope_tables(T, IDX_DIM)
    cq = jnp.concatenate([cos_a, cos_a], -1)
    sq = jnp.concatenate([-sin_a, sin_a], -1)
    ci = jnp.concatenate([cos_i, cos_i, cos_i, cos_i], -1)
    si = jnp.concatenate([-sin_i, sin_i, -sin_i, sin_i], -1)

    def qrow(width, col0):
        return pl.BlockSpec((Q_BLOCK, width), lambda b, j: (b * nq + j, col0 // width))

    def krow(col0):
        return pl.BlockSpec((T, LANES), lambda b, j: (b, col0 // LANES))

    tab_q = pl.BlockSpec((Q_BLOCK, LANES), lambda b, j: (j, 0))
    tab_k = pl.BlockSpec((T, LANES), lambda b, j: (0, 0))
    return pl.pallas_call(
        functools.partial(_dsa_body, n_sel),
        grid=(B, nq),
        in_specs=[qrow(ATT_HEADS * ATT_HEAD, DSA_Q0), qrow(IDX_HEADS * IDX_DIM, DSA_QI0), qrow(LANES, DSA_KIW0),
                  krow(DSA_K0), krow(DSA_V0), krow(DSA_KIW0),
                  tab_q, tab_q, tab_q, tab_q, tab_k, tab_k, tab_k, tab_k],
        out_specs=pl.BlockSpec((Q_BLOCK, D_MODEL), lambda b, j: (b * nq + j, 0)),
        out_shape=jax.ShapeDtypeStruct((B * T, D_MODEL), F32),
        scratch_shapes=[pltpu.VMEM((T, ATT_HEAD), jnp.bfloat16), pltpu.VMEM((T, ATT_HEAD), jnp.bfloat16),
                        pltpu.VMEM((T, LANES), jnp.bfloat16)],
        compiler_params=pltpu.CompilerParams(dimension_semantics=("arbitrary", "arbitrary"),
                                             vmem_limit_bytes=VMEM_LIMIT),
        name="dsa_attention",
    )(p, p, p, p, p, p, cq, sq, ci, si, cq, sq, ci, si)


def dsa_mixer(x, w_in, w_out):
    B, T, D = x.shape
    wq, wk, wv, wqi, wki, wwi = _split(w_in, ODD_SPLITS)
    pad = jnp.zeros((D, DSA_COLS - DSA_KIW0 - IDX_DIM - IDX_HEADS), w_in.dtype)
    w_cat = jnp.concatenate([wq, wqi, wk, wv, wki, wwi, pad], axis=1)
    p = matmul(x.reshape(B * T, D), w_cat)
    o = dsa_attention(p, B, T)
    return matmul(o, w_out).reshape(B, T, D)


NPAIR = PEER_HEADS * PEER_TOPK
PEER_TB = 128
SLAB = 4
GS = 136
PAIRS_PER_ITER = 4
RT_TOKENS = 256
PEER_VMEM_LIMIT = 56 * 1024 * 1024
NEG_INF = float("-inf")


def pack_table(tab):
    E = tab.shape[0]
    b = lax.bitcast_convert_type(tab.astype(jnp.bfloat16), jnp.uint16).astype(jnp.uint32)
    b = b.reshape(E, SLAB, 2, LANES)
    w = b[:, :, 0, :] | (b[:, :, 1, :] << 16)
    return lax.bitcast_convert_type(w, jnp.int32).reshape(E * SLAB, LANES)


def _gather_rows(idx_ref, base, tab_ref, gt_ref):
    for p in range(NPAIR):
        i = pl.multiple_of(idx_ref[0, 0, base + p], SLAB)
        gt_ref[pl.ds(p, SLAB, stride=GS), :] = tab_ref[pl.ds(i, SLAB), :]


def _unpacked(gt_ref):
    parts = []
    for j in range(SLAB):
        w = gt_ref[pl.ds(j * GS, NPAIR), :]
        parts.append(pltpu.bitcast(w << 16, F32).astype(jnp.bfloat16))
        parts.append(pltpu.bitcast(w & jnp.int32(-65536), F32).astype(jnp.bfloat16))
    return jnp.concatenate(parts, axis=1)


def _token_loop(idx_ref, tab_ref, gta_ref, gtb_ref, compute):
    _gather_rows(idx_ref, 0, tab_ref, gta_ref)

    def pairs(i, c):
        for k in range(PAIRS_PER_ITER):
            t = 2 * (i * PAIRS_PER_ITER + k)
            _gather_rows(idx_ref, (t + 1) * NPAIR, tab_ref, gtb_ref)
            compute(t, _unpacked(gta_ref))
            _gather_rows(idx_ref, jnp.minimum(t + 2, PEER_TB - 1) * NPAIR, tab_ref, gta_ref)
            compute(t + 1, _unpacked(gtb_ref))
        return c

    lax.fori_loop(0, PEER_TB // (2 * PAIRS_PER_ITER), pairs, 0)


def _peer_u_body(idx_ref, x_ref, tab_ref, h_ref, gta_ref, gtb_ref):
    def compute(t, g):
        xt = x_ref[pl.ds(t, 1), :].astype(jnp.bfloat16)
        h_ref[pl.ds(t, 1), :] = lax.dot_general(xt, g, (((1,), (1,)), ((), ())), preferred_element_type=F32)

    _token_loop(idx_ref, tab_ref, gta_ref, gtb_ref, compute)


def _peer_v_body(idx_ref, h_ref, gate_ref, tab_ref, o_ref, coef_ref, gta_ref, gtb_ref):
    hv = h_ref[...]
    coef_ref[...] = gate_ref[...] * (0.5 * hv * (1.0 + lax.erf(hv * (2.0 ** -0.5))))

    def compute(t, g):
        ct = coef_ref[pl.ds(t, 1), :].astype(jnp.bfloat16)
        o_ref[pl.ds(t, 1), :] = jnp.dot(ct, g, preferred_element_type=F32)

    _token_loop(idx_ref, tab_ref, gta_ref, gtb_ref, compute)


def _peer_gather_call(body, name, idx4, acts, tab, out_w, extra_scratch):
    n = acts[0].shape[0]
    nb = n // PEER_TB
    gather_tile = pltpu.VMEM((SLAB * GS, LANES), jnp.int32)
    return pl.pallas_call(
        body,
        grid=(nb,),
        in_specs=[pl.BlockSpec((1, 1, PEER_TB * NPAIR), lambda i: (i, 0, 0), memory_space=pltpu.SMEM)]
        + [pl.BlockSpec((PEER_TB, a.shape[1]), lambda i: (i, 0)) for a in acts]
        + [pl.BlockSpec(memory_space=pltpu.VMEM)],
        out_specs=pl.BlockSpec((PEER_TB, out_w), lambda i: (i, 0)),
        out_shape=jax.ShapeDtypeStruct((n, out_w), F32),
        scratch_shapes=extra_scratch + [gather_tile, gather_tile],
        compiler_params=pltpu.CompilerParams(dimension_semantics=("arbitrary",),
                                             vmem_limit_bytes=PEER_VMEM_LIMIT),
        name=name,
    )(idx4.reshape(nb, 1, PEER_TB * NPAIR), *acts, tab)


def _top16(vals, ids):
    n, T = vals.shape
    pos = lax.broadcasted_iota(jnp.int32, (n, T), 0)
    slot = lax.broadcasted_iota(jnp.int32, (PEER_TOPK, T), 0)
    top_v = jnp.zeros((PEER_TOPK, T), F32)
    top_i = jnp.zeros((PEER_TOPK, T), jnp.int32)
    for r in range(PEER_TOPK):
        m = jnp.max(vals, axis=0, keepdims=True)
        am = jnp.min(jnp.where(vals == m, pos, n), axis=0, keepdims=True)
        hit = pos == am
        sel = jnp.max(jnp.where(hit, ids, -1), axis=0, keepdims=True)
        top_v = jnp.where(slot == r, m, top_v)
        top_i = jnp.where(slot == r, sel, top_i)
        vals = jnp.where(hit, NEG_INF, vals)
    return top_v, top_i


def _route_body(x_ref, wqt_ref, keys_ref, idx_ref, gate_ref):
    T = x_ref.shape[0]
    qt = lax.dot_general(wqt_ref[...], x_ref[...].astype(jnp.bfloat16), (((1,), (1,)), ((), ())),
                         preferred_element_type=F32)
    half = PEER_DKEY // 2
    key_id = lax.broadcasted_iota(jnp.int32, (PEER_NKEYS, T), 0)
    idx_rows, gate_rows = [], []
    for h in range(PEER_HEADS):
        tops = []
        for c in range(2):
            g = h * 2 + c
            s = jnp.dot(keys_ref[g], qt[g * half:(g + 1) * half, :].astype(jnp.bfloat16),
                        preferred_element_type=F32)
            tops.append(_top16(s, key_id))
        (s0, i0), (s1, i1) = tops
        cand_s = jnp.concatenate([s0[a:a + 1, :] + s1 for a in range(PEER_TOPK)], axis=0)
        cand_i = jnp.concatenate([i0[a:a + 1, :] * PEER_NKEYS + i1 for a in range(PEER_TOPK)], axis=0)
        best_s, best_i = _top16(cand_s, cand_i)
        e = jnp.exp(best_s - best_s[0:1, :])
        gate_rows.append(e / jnp.sum(e, axis=0, keepdims=True))
        idx_rows.append(best_i * SLAB)
    idx_ref[...] = jnp.concatenate(idx_rows, axis=0).T
    gate_ref[...] = jnp.concatenate(gate_rows, axis=0).T


def peer_route(x, w_q, sub_keys):
    n = x.shape[0]
    tb = min(RT_TOKENS, n)
    wqt = w_q.T.astype(jnp.bfloat16)
    keys = sub_keys.reshape(PEER_HEADS * 2, PEER_NKEYS, PEER_DKEY // 2).astype(jnp.bfloat16)
    return pl.pallas_call(
        _route_body,
        grid=(n // tb,),
        in_specs=[pl.BlockSpec((tb, D_MODEL), lambda i: (i, 0)),
                  pl.BlockSpec(wqt.shape, lambda i: (0, 0)),
                  pl.BlockSpec(keys.shape, lambda i: (0, 0, 0))],
        out_specs=[pl.BlockSpec((tb, NPAIR), lambda i: (i, 0)), pl.BlockSpec((tb, NPAIR), lambda i: (i, 0))],
        out_shape=[jax.ShapeDtypeStruct((n, NPAIR), jnp.int32), jax.ShapeDtypeStruct((n, NPAIR), F32)],
        compiler_params=pltpu.CompilerParams(dimension_semantics=("arbitrary",), vmem_limit_bytes=PEER_VMEM_LIMIT),
        name="peer_route",
    )(x, wqt, keys)


def peer_ffn(x, w_q, sub_keys, exp_u, exp_v):
    B, T, D = x.shape
    xf = x.reshape(B * T, D)
    idx4, gate = peer_route(xf, w_q, sub_keys)
    h = _peer_gather_call(_peer_u_body, "peer_u", idx4, [xf], pack_table(exp_u), NPAIR, [])
    out = _peer_gather_call(_peer_v_body, "peer_v", idx4, [h, gate], pack_table(exp_v), D,
                            [pltpu.VMEM((PEER_TB, NPAIR), F32)])
    return out.reshape(B, T, D)


def kernel(x, even_w_in, gla_a_w2, gla_a_b, gla_norm_g, rwkv_mu, rwkv_w0, rwkv_w2, rwkv_a0, rwkv_a2,
           rwkv_g2, rwkv_k_k, rwkv_k_a, rwkv_r_k, rwkv_lnx_g, rwkv_lnx_b, even_w_out, odd_w_in,
           odd_w_out, mix_ln_g, mix_ln_b, peer_w_q, peer_sub_keys, peer_u, peer_v, ffn_ln_g, ffn_ln_b):
    for layer in range(DEPTH):
        i = layer // 2
        if layer % 2 == 0:
            mix = even_mixer(x, even_w_in[i], gla_a_w2[i], gla_a_b[i], gla_norm_g[i], rwkv_mu[i], rwkv_w0[i],
                             rwkv_w2[i], rwkv_a0[i], rwkv_a2[i], rwkv_g2[i], rwkv_k_k[i], rwkv_k_a[i],
                             rwkv_r_k[i], rwkv_lnx_g[i], rwkv_lnx_b[i], even_w_out[i])
        else:
            mix = dsa_mixer(x, odd_w_in[i], odd_w_out[i])
        x = add_layer_norm(x, mix, mix_ln_g[layer], mix_ln_b[layer])
        ffn = peer_ffn(x, peer_w_q[layer], peer_sub_keys[layer], peer_u[layer], peer_v[layer])
        x = add_layer_norm(x, ffn, ffn_ln_g[layer], ffn_ln_b[layer])
    return x
```

```python
import functools

import jax
import jax.numpy as jnp
import numpy as np
from jax import lax
from jax.experimental import pallas as pl
from jax.experimental.pallas import tpu as pltpu

D_MODEL = 1024
BATCH = 32
SEQ = 2048
DEPTH = 4

GLA_HEADS = 4
GLA_VW = D_MODEL // 2
GLA_DV = GLA_VW // GLA_HEADS
GLA_KW = GLA_VW // 2
GLA_DK = GLA_KW // GLA_HEADS
GLA_LORA = 16
GLA_TAU = 16.0
GLA_CHUNK = 64
RWKV_W = D_MODEL - GLA_VW
RWKV_HEAD = 64
RWKV_HEADS = RWKV_W // RWKV_HEAD
RWKV_W_LORA = 32
RWKV_A_LORA = 32
RWKV_G_LORA = 96
RWKV_LNX_EPS = 64e-5
ATT_HEADS = 8
ATT_HEAD = D_MODEL // ATT_HEADS
IDX_HEADS = 8
IDX_DIM = 64
TOPK_MAX = 256
Q_BLOCK = 128
ROPE_THETA = 10000.0
NEG = -1e30
PEER_HEADS = 8
PEER_DKEY = 128
PEER_NKEYS = 128
PEER_N_EXPERTS = PEER_NKEYS * PEER_NKEYS
PEER_TOPK = 16
PEER_TOKEN_BLOCK = 128
DN_ALPHA = (2 * DEPTH) ** 0.25
DN_BETA = (8 * DEPTH) ** -0.25
LN_EPS = 1e-5
N_EVEN = (DEPTH + 1) // 2
N_ODD = DEPTH // 2

GLA_SPLITS = (GLA_KW, GLA_KW, GLA_VW, GLA_VW, GLA_LORA)
RWKV_SPLITS = (RWKV_W, RWKV_W, RWKV_W, RWKV_W_LORA, RWKV_A_LORA, RWKV_G_LORA)
GLA_IN = sum(GLA_SPLITS)
RWKV_IN = sum(RWKV_SPLITS)
EVEN_IN = GLA_IN + RWKV_IN
ODD_SPLITS = (ATT_HEADS * ATT_HEAD, ATT_HEAD, ATT_HEAD, IDX_HEADS * IDX_DIM, IDX_DIM, IDX_HEADS)
ODD_IN = sum(ODD_SPLITS)

F32 = jnp.float32


def _split(t, sizes):
    cuts = [int(c) for c in np.cumsum(sizes)[:-1]]
    return jnp.split(t, cuts, axis=-1)


LN_ROWS = 512


def _add_ln_body(x_ref, y_ref, g_ref, b_ref, o_ref):
    z = DN_ALPHA * x_ref[...] + y_ref[...]
    mu = jnp.mean(z, axis=-1, keepdims=True)
    zc = z - mu
    var = jnp.mean(zc * zc, axis=-1, keepdims=True)
    o_ref[...] = zc * lax.rsqrt(var + LN_EPS) * g_ref[...] + b_ref[...]


def add_layer_norm(x, y, g, b):
    B, T, D = x.shape
    n = B * T
    row = pl.BlockSpec((LN_ROWS, D), lambda i: (i, 0))
    vec = pl.BlockSpec((1, D), lambda i: (0, 0))
    out = pl.pallas_call(
        _add_ln_body,
        grid=(n // LN_ROWS,),
        in_specs=[row, row, vec, vec],
        out_specs=row,
        out_shape=jax.ShapeDtypeStruct((n, D), F32),
        name="add_layer_norm",
    )(x.reshape(n, D), y.reshape(n, D), g.reshape(1, D), b.reshape(1, D))
    return out.reshape(B, T, D)


def rope_tables(T, dim):
    inv = ROPE_THETA ** (-jnp.arange(0, dim, 2, dtype=F32) / dim)
    ang = jnp.arange(T, dtype=F32)[:, None] * inv[None, :]
    return jnp.cos(ang), jnp.sin(ang)


def apply_rope(x, cos, sin):
    x1, x2 = jnp.split(x, 2, axis=-1)
    return jnp.concatenate([x1 * cos - x2 * sin, x2 * cos + x1 * sin], -1).astype(x.dtype)


def gla_chunked(q, k, v, log_a):
    B, T, H, dk = q.shape
    dv = v.shape[-1]
    C = GLA_CHUNK
    n = T // C

    def chunks(t):
        return t.astype(F32).reshape(B, n, C, H, -1).transpose(0, 3, 1, 2, 4)

    q, k, v, la = chunks(q) * dk ** -0.5, chunks(k), chunks(v), chunks(log_a)
    b = jnp.cumsum(la, axis=3)
    b_last = b[:, :, :, -1:, :]
    q_in = q * jnp.exp(b)
    k_in = k * jnp.exp(-b)
    k_out = k * jnp.exp(b_last - b)
    causal = jnp.tril(jnp.ones((C, C), dtype=bool))
    A = jnp.where(causal, jnp.einsum('bhnid,bhnjd->bhnij', q_in, k_in), 0.0)
    o = jnp.einsum('bhnij,bhnje->bhnie', A, v)
    kv = jnp.einsum('bhnjd,bhnje->bhnde', k_out, v)
    dec = jnp.exp(b_last[:, :, :, 0, :])

    def step(S, inp):
        dec_n, kv_n = inp
        return dec_n[..., None] * S + kv_n, S

    S0 = jnp.zeros((B, H, dk, dv), F32)
    _, S_prev = lax.scan(step, S0, (jnp.moveaxis(dec, 2, 0), jnp.moveaxis(kv, 2, 0)))
    S_prev = jnp.moveaxis(S_prev, 0, 2)
    o = o + jnp.einsum('bhnid,bhnde->bhnie', q_in, S_prev)
    return o.transpose(0, 2, 3, 1, 4).reshape(B, T, H, dv)


SCAN_TC = 16


def _rwkv_scan_body(r_ref, w_ref, k_ref, v_ref, a_ref, b_ref, y_ref, s_ref):
    N = RWKV_HEAD

    @pl.when(pl.program_id(1) == 0)
    def _():
        s_ref[...] = jnp.zeros_like(s_ref)

    def step(tt, c):
        sa = jnp.zeros((N, LANES), F32)
        for j in range(N):
            sa = sa + s_ref[j] * a_ref[tt, j:j + 1, :]
        v_t = v_ref[tt]
        y = jnp.zeros((N, LANES), F32)
        for j in range(N):
            sj = s_ref[j] * w_ref[tt, j:j + 1, :] + sa * b_ref[tt, j:j + 1, :] + v_t * k_ref[tt, j:j + 1, :]
            s_ref[j] = sj
            y = y + sj * r_ref[tt, j:j + 1, :]
        y_ref[tt] = y
        return c

    lax.fori_loop(0, r_ref.shape[0], step, 0)


def rwkv7_scan(r, w, k, v, a, b):
    B, T, H, N = r.shape
    C = B * H
    tc = min(SCAN_TC, T)

    def to_scan(t):
        return t.astype(F32).transpose(1, 3, 0, 2).reshape(T, N, C)

    spec = pl.BlockSpec((tc, N, LANES), lambda g, i: (i, 0, g))
    y = pl.pallas_call(
        _rwkv_scan_body,
        grid=(C // LANES, T // tc),
        in_specs=[spec] * 6,
        out_specs=spec,
        out_shape=jax.ShapeDtypeStruct((T, N, C), F32),
        scratch_shapes=[pltpu.VMEM((N, N, LANES), F32)],
        compiler_params=pltpu.CompilerParams(dimension_semantics=("arbitrary", "arbitrary"),
                                             vmem_limit_bytes=VMEM_LIMIT),
        name="rwkv7_scan",
    )(*(to_scan(t) for t in (r, w, k, v, a, b)))
    return y.reshape(T, N, B, H).transpose(2, 0, 3, 1)


def even_mixer(x, w_in, gla_a_w2, gla_a_b, gla_norm_g, rwkv_mu, rwkv_w0, rwkv_w2, rwkv_a0, rwkv_a2,
               rwkv_g2, rwkv_k_k, rwkv_k_a, rwkv_r_k, rwkv_lnx_g, rwkv_lnx_b, w_out):
    B, T, _ = x.shape

    def heads(t, h):
        return t.reshape(B, T, h, -1)

    p = matmul(x.reshape(B * T, -1), w_in).reshape(B, T, -1)
    gla_p, rw_p = p[..., :GLA_IN], p[..., GLA_IN:]
    gq, gk, gv, gg, gal = _split(gla_p, GLA_SPLITS)
    log_a = jax.nn.log_sigmoid((gal @ gla_a_w2 + gla_a_b).astype(F32)) / GLA_TAU
    o = gla_chunked(heads(gq, GLA_HEADS), heads(gk, GLA_HEADS), heads(gv, GLA_HEADS), heads(log_a, GLA_HEADS))
    o = o * lax.rsqrt(jnp.mean(jnp.square(o), -1, keepdims=True) + LN_EPS) * gla_norm_g
    gla_out = (o.reshape(B, T, GLA_VW) * jax.nn.silu(gg.astype(F32))).astype(x.dtype)
    prev = jnp.pad(rw_p, ((0, 0), (1, 0), (0, 0)))[:, :-1]
    rw_p = rw_p + (prev - rw_p) * rwkv_mu
    r, k, v, wl, al, gl = _split(rw_p, RWKV_SPLITS)
    w_log = -jax.nn.softplus(-(rwkv_w0 + jnp.tanh(wl) @ rwkv_w2).astype(F32)) - 0.5
    decay = jnp.exp(-jnp.exp(w_log))
    a = jax.nn.sigmoid((rwkv_a0 + al @ rwkv_a2).astype(F32))
    g = jax.nn.sigmoid(gl) @ rwkv_g2
    kk = heads((k * rwkv_k_k).astype(F32), RWKV_HEADS)
    kk = kk / jnp.maximum(jnp.sqrt(jnp.sum(jnp.square(kk), -1, keepdims=True)), 1e-12)
    k = k * (1.0 + (a - 1.0) * rwkv_k_a)
    a_h = heads(a, RWKV_HEADS)
    r_h, k_h, v_h = heads(r, RWKV_HEADS), heads(k, RWKV_HEADS), heads(v, RWKV_HEADS)
    y = rwkv7_scan(r_h, heads(decay, RWKV_HEADS), k_h, v_h, -kk, kk * a_h)
    mu = y.mean(-1, keepdims=True)
    var = jnp.square(y - mu).mean(-1, keepdims=True)
    y = (y - mu) * lax.rsqrt(var + RWKV_LNX_EPS) * rwkv_lnx_g.reshape(RWKV_HEADS, RWKV_HEAD) \
        + rwkv_lnx_b.reshape(RWKV_HEADS, RWKV_HEAD)
    bonus = jnp.sum((r_h * k_h * rwkv_r_k).astype(F32), -1, keepdims=True) * v_h.astype(F32)
    rwkv_out = ((y + bonus).reshape(B, T, RWKV_W) * g.astype(F32)).astype(x.dtype)
    return matmul(jnp.concatenate([gla_out, rwkv_out], -1).reshape(B * T, -1), w_out).reshape(B, T, -1)


MM_ROWS = 512
VMEM_LIMIT = 48 * 1024 * 1024


def _mm_body(x_ref, w_ref, o_ref):
    o_ref[...] = jnp.dot(x_ref[...].astype(jnp.bfloat16), w_ref[...], preferred_element_type=F32)


def matmul(x, w):
    M, K = x.shape
    N = w.shape[1]
    tm = min(MM_ROWS, M)
    return pl.pallas_call(
        _mm_body,
        grid=(M // tm,),
        in_specs=[pl.BlockSpec((tm, K), lambda i: (i, 0)), pl.BlockSpec((K, N), lambda i: (0, 0))],
        out_specs=pl.BlockSpec((tm, N), lambda i: (i, 0)),
        out_shape=jax.ShapeDtypeStruct((M, N), F32),
        compiler_params=pltpu.CompilerParams(vmem_limit_bytes=VMEM_LIMIT),
        name="matmul",
    )(x, w.astype(jnp.bfloat16))


LANES = 128
INT_MIN = -2 ** 31
DSA_Q0 = 0
DSA_QI0 = ATT_HEADS * ATT_HEAD
DSA_K0 = DSA_QI0 + IDX_HEADS * IDX_DIM
DSA_V0 = DSA_K0 + ATT_HEAD
DSA_KIW0 = DSA_V0 + ATT_HEAD
DSA_COLS = DSA_KIW0 + LANES


def _swap32(x):
    lane = lax.broadcasted_iota(jnp.int32, x.shape, 1)
    return jnp.where((lane % IDX_DIM) < IDX_DIM // 2, pltpu.roll(x, LANES - IDX_DIM // 2, 1),
                     pltpu.roll(x, IDX_DIM // 2, 1))


def _count_ge(key, cand):
    return jnp.sum(jnp.where(key >= cand, 1.0, 0.0), axis=1, keepdims=True)


def _dsa_body(n_sel, q_ref, qi_ref, kiwq_ref, k_ref, v_ref, kiw_ref, cqb_ref, sqb_ref, cib_ref, sib_ref,
              cq_ref, sq_ref, ci_ref, si_ref, o_ref, krot_ref, vbf_ref, kirot_ref):
    j = pl.program_id(1)
    T = k_ref.shape[0]
    QB = q_ref.shape[0]

    @pl.when(j == 0)
    def _():
        k = k_ref[...]
        krot_ref[...] = (k * cq_ref[...] + pltpu.roll(k, ATT_HEAD // 2, 1) * sq_ref[...]).astype(jnp.bfloat16)
        vbf_ref[...] = v_ref[...].astype(jnp.bfloat16)
        kiw = kiw_ref[...]
        kirot_ref[...] = (kiw * ci_ref[...] + _swap32(kiw) * si_ref[...]).astype(jnp.bfloat16)

    ci_b = cib_ref[...]
    si_b = sib_ref[...]
    ki = kirot_ref[:, :IDX_DIM]
    wi = kiwq_ref[:, IDX_DIM:IDX_DIM + IDX_HEADS] * (IDX_HEADS ** -0.5 * IDX_DIM ** -0.5)
    score = jnp.zeros((QB, T), F32)
    for g in range(IDX_HEADS * IDX_DIM // LANES):
        qg = qi_ref[:, g * LANES:(g + 1) * LANES]
        qg = (qg * ci_b + _swap32(qg) * si_b).astype(jnp.bfloat16)
        for hh in range(LANES // IDX_DIM):
            h = g * (LANES // IDX_DIM) + hh
            s = lax.dot_general(qg[:, hh * IDX_DIM:(hh + 1) * IDX_DIM], ki, (((1,), (1,)), ((), ())),
                                preferred_element_type=F32)
            score = score + wi[:, h:h + 1] * jnp.maximum(s, 0.0)
    col = lax.broadcasted_iota(jnp.int32, (QB, T), 1)
    row = j * QB + lax.broadcasted_iota(jnp.int32, (QB, T), 0)
    causal = col <= row
    score = jnp.where(causal, score, NEG)

    bits = pltpu.bitcast(score, jnp.int32)
    key = jnp.where(bits < 0, bits ^ jnp.int32(0x7FFFFFFF), bits)
    kf = float(n_sel)
    t0 = jnp.where(_count_ge(key, jnp.zeros((QB, 1), jnp.int32)) >= kf, jnp.int32(0), jnp.int32(INT_MIN))

    def bit_step(i, t):
        cand = t | (jnp.int32(1) << (30 - i))
        return jnp.where(_count_ge(key, cand) >= kf, cand, t)

    thr = lax.fori_loop(0, 31, bit_step, t0)
    gt = key > thr
    eq = key == thr
    need = kf - jnp.sum(jnp.where(gt, 1.0, 0.0), axis=1, keepdims=True)

    def tie_step(i, jj):
        cand = jj | (jnp.int32(1) << (T.bit_length() - 2 - i))
        c = jnp.sum(jnp.where(eq & (col < cand), 1.0, 0.0), axis=1, keepdims=True)
        return jnp.where(c < need, cand, jj)

    jstar = lax.fori_loop(0, T.bit_length() - 1, tie_step, jnp.zeros((QB, 1), jnp.int32))
    sel = (gt | (eq & (col <= jstar))) & causal
    bias = jnp.where(sel, 0.0, NEG)

    cq_b = cqb_ref[...]
    sq_b = sqb_ref[...]
    kr = krot_ref[...]
    vb = vbf_ref[...]
    for h in range(ATT_HEADS):
        qh = q_ref[:, h * ATT_HEAD:(h + 1) * ATT_HEAD]
        qh = (qh * cq_b + pltpu.roll(qh, ATT_HEAD // 2, 1) * sq_b).astype(jnp.bfloat16)
        logit = lax.dot_general(qh, kr, (((1,), (1,)), ((), ())), preferred_element_type=F32)
        logit = logit * (ATT_HEAD ** -0.5) + bias
        m = jnp.max(logit, axis=1, keepdims=True)
        p = jnp.exp(logit - m)
        l = jnp.sum(p, axis=1, keepdims=True)
        o = jnp.dot(p.astype(jnp.bfloat16), vb, preferred_element_type=F32)
        o_ref[:, h * ATT_HEAD:(h + 1) * ATT_HEAD] = o / l


def dsa_attention(p, B, T):
    n_sel = min(TOPK_MAX, T // 4)
    nq = T // Q_BLOCK
    cos_a, sin_a = rope_tables(T, ATT_HEAD)
    cos_i, sin_i = rope_tables(T, IDX_DIM)
    cq = jnp.concatenate([cos_a, cos_a], -1)
    sq = jnp.concatenate([-sin_a, sin_a], -1)
    ci = jnp.concatenate([cos_i, cos_i, cos_i, cos_i], -1)
    si = jnp.concatenate([-sin_i, sin_i, -sin_i, sin_i], -1)

    def qrow(width, col0):
        return pl.BlockSpec((Q_BLOCK, width), lambda b, j: (b * nq + j, col0 // width))

    def krow(col0):
        return pl.BlockSpec((T, LANES), lambda b, j: (b, col0 // LANES))

    tab_q = pl.BlockSpec((Q_BLOCK, LANES), lambda b, j: (j, 0))
    tab_k = pl.BlockSpec((T, LANES), lambda b, j: (0, 0))
    return pl.pallas_call(
        functools.partial(_dsa_body, n_sel),
        grid=(B, nq),
        in_specs=[qrow(ATT_HEADS * ATT_HEAD, DSA_Q0), qrow(IDX_HEADS * IDX_DIM, DSA_QI0), qrow(LANES, DSA_KIW0),
                  krow(DSA_K0), krow(DSA_V0), krow(DSA_KIW0),
                  tab_q, tab_q, tab_q, tab_q, tab_k, tab_k, tab_k, tab_k],
        out_specs=pl.BlockSpec((Q_BLOCK, D_MODEL), lambda b, j: (b * nq + j, 0)),
        out_shape=jax.ShapeDtypeStruct((B * T, D_MODEL), F32),
        scratch_shapes=[pltpu.VMEM((T, ATT_HEAD), jnp.bfloat16), pltpu.VMEM((T, ATT_HEAD), jnp.bfloat16),
                        pltpu.VMEM((T, LANES), jnp.bfloat16)],
        compiler_params=pltpu.CompilerParams(dimension_semantics=("arbitrary", "arbitrary"),
                                             vmem_limit_bytes=VMEM_LIMIT),
        name="dsa_attention",
    )(p, p, p, p, p, p, cq, sq, ci, si, cq, sq, ci, si)


def dsa_mixer(x, w_in, w_out):
    B, T, D = x.shape
    wq, wk, wv, wqi, wki, wwi = _split(w_in, ODD_SPLITS)
    pad = jnp.zeros((D, DSA_COLS - DSA_KIW0 - IDX_DIM - IDX_HEADS), w_in.dtype)
    w_cat = jnp.concatenate([wq, wqi, wk, wv, wki, wwi, pad], axis=1)
    p = matmul(x.reshape(B * T, D), w_cat)
    o = dsa_attention(p, B, T)
    return matmul(o, w_out).reshape(B, T, D)


NPAIR = PEER_HEADS * PEER_TOPK
PEER_TB = 128
SLAB = 4
GS = 136
PAIRS_PER_ITER = 4
RT_TOKENS = 256
PEER_VMEM_LIMIT = 56 * 1024 * 1024
NEG_INF = float("-inf")


def pack_table(tab):
    E = tab.shape[0]
    b = lax.bitcast_convert_type(tab.astype(jnp.bfloat16), jnp.uint16).astype(jnp.uint32)
    b = b.reshape(E, SLAB, 2, LANES)
    w = b[:, :, 0, :] | (b[:, :, 1, :] << 16)
    return lax.bitcast_convert_type(w, jnp.int32).reshape(E * SLAB, LANES)


def _gather_rows(idx_ref, base, tab_ref, gt_ref):
    for p in range(NPAIR):
        i = pl.multiple_of(idx_ref[0, 0, base + p], SLAB)
        gt_ref[pl.ds(p, SLAB, stride=GS), :] = tab_ref[pl.ds(i, SLAB), :]


def _unpacked(gt_ref):
    parts = []
    for j in range(SLAB):
        w = gt_ref[pl.ds(j * GS, NPAIR), :]
        parts.append(pltpu.bitcast(w << 16, F32).astype(jnp.bfloat16))
        parts.append(pltpu.bitcast(w & jnp.int32(-65536), F32).astype(jnp.bfloat16))
    return jnp.concatenate(parts, axis=1)


def _token_loop(idx_ref, tab_ref, gta_ref, gtb_ref, compute):
    _gather_rows(idx_ref, 0, tab_ref, gta_ref)

    def pairs(i, c):
        for k in range(PAIRS_PER_ITER):
            t = 2 * (i * PAIRS_PER_ITER + k)
            _gather_rows(idx_ref, (t + 1) * NPAIR, tab_ref, gtb_ref)
            compute(t, _unpacked(gta_ref))
            _gather_rows(idx_ref, jnp.minimum(t + 2, PEER_TB - 1) * NPAIR, tab_ref, gta_ref)
            compute(t + 1, _unpacked(gtb_ref))
        return c

    lax.fori_loop(0, PEER_TB // (2 * PAIRS_PER_ITER), pairs, 0)


def _peer_u_body(idx_ref, x_ref, tab_ref, h_ref, gta_ref, gtb_ref):
    def compute(t, g):
        xt = x_ref[pl.ds(t, 1), :].astype(jnp.bfloat16)
        h_ref[pl.ds(t, 1), :] = lax.dot_general(xt, g, (((1,), (1,)), ((), ())), preferred_element_type=F32)

    _token_loop(idx_ref, tab_ref, gta_ref, gtb_ref, compute)


def _peer_v_body(idx_ref, h_ref, gate_ref, tab_ref, o_ref, coef_ref, gta_ref, gtb_ref):
    hv = h_ref[...]
    coef_ref[...] = gate_ref[...] * (0.5 * hv * (1.0 + lax.erf(hv * (2.0 ** -0.5))))

    def compute(t, g):
        ct = coef_ref[pl.ds(t, 1), :].astype(jnp.bfloat16)
        o_ref[pl.ds(t, 1), :] = jnp.dot(ct, g, preferred_element_type=F32)

    _token_loop(idx_ref, tab_ref, gta_ref, gtb_ref, compute)


def _peer_gather_call(body, name, idx4, acts, tab, out_w, extra_scratch):
    n = acts[0].shape[0]
    nb = n // PEER_TB
    gather_tile = pltpu.VMEM((SLAB * GS, LANES), jnp.int32)
    return pl.pallas_call(
        body,
        grid=(nb,),
        in_specs=[pl.BlockSpec((1, 1, PEER_TB * NPAIR), lambda i: (i, 0, 0), memory_space=pltpu.SMEM)]
        + [pl.BlockSpec((PEER_TB, a.shape[1]), lambda i: (i, 0)) for a in acts]
        + [pl.BlockSpec(memory_space=pltpu.VMEM)],
        out_specs=pl.BlockSpec((PEER_TB, out_w), lambda i: (i, 0)),
        out_shape=jax.ShapeDtypeStruct((n, out_w), F32),
        scratch_shapes=extra_scratch + [gather_tile, gather_tile],
        compiler_params=pltpu.CompilerParams(dimension_semantics=("arbitrary",),
                                             vmem_limit_bytes=PEER_VMEM_LIMIT),
        name=name,
    )(idx4.reshape(nb, 1, PEER_TB * NPAIR), *acts, tab)


def _top16(vals, ids):
    n, T = vals.shape
    pos = lax.broadcasted_iota(jnp.int32, (n, T), 0)
    slot = lax.broadcasted_iota(jnp.int32, (PEER_TOPK, T), 0)
    top_v = jnp.zeros((PEER_TOPK, T), F32)
    top_i = jnp.zeros((PEER_TOPK, T), jnp.int32)
    for r in range(PEER_TOPK):
        m = jnp.max(vals, axis=0, keepdims=True)
        am = jnp.min(jnp.where(vals == m, pos, n), axis=0, keepdims=True)
        hit = pos == am
        sel = am if ids is None else jnp.max(jnp.where(hit, ids, -1), axis=0, keepdims=True)
        top_v = jnp.where(slot == r, m, top_v)
        top_i = jnp.where(slot == r, sel, top_i)
        vals = jnp.where(hit, NEG_INF, vals)
    return top_v, top_i


def _route_body(x_ref, wqt_ref, keys_ref, idx_ref, gate_ref):
    T = x_ref.shape[0]
    qt = lax.dot_general(wqt_ref[...], x_ref[...].astype(jnp.bfloat16), (((1,), (1,)), ((), ())),
                         preferred_element_type=F32)
    half = PEER_DKEY // 2
    idx_rows, gate_rows = [], []
    for h in range(PEER_HEADS):
        tops = []
        for c in range(2):
            g = h * 2 + c
            s = jnp.dot(keys_ref[g], qt[g * half:(g + 1) * half, :].astype(jnp.bfloat16),
                        preferred_element_type=F32)
            tops.append(_top16(s, None))
        (s0, i0), (s1, i1) = tops
        sub8 = lax.broadcasted_iota(jnp.int32, (8, T), 0)
        cs, cids = [s0[0:1, :] + s1], [i0[0:1, :] * PEER_NKEYS + i1]
        for a in range(1, 8):
            keep = sub8 < PEER_TOPK // (a + 1)
            cs.append(jnp.where(keep, s0[a:a + 1, :] + s1[0:8, :], NEG_INF))
            cids.append(i0[a:a + 1, :] * PEER_NKEYS + i1[0:8, :])
        cs.append(s0[8:16, :] + s1[0:1, :])
        cids.append(i0[8:16, :] * PEER_NKEYS + i1[0:1, :])
        best_s, best_i = _top16(jnp.concatenate(cs, axis=0), jnp.concatenate(cids, axis=0))
        e = jnp.exp(best_s - best_s[0:1, :])
        gate_rows.append(e / jnp.sum(e, axis=0, keepdims=True))
        idx_rows.append(best_i * SLAB)
    idx_ref[...] = jnp.concatenate(idx_rows, axis=0).T
    gate_ref[...] = jnp.concatenate(gate_rows, axis=0).T


def peer_route(x, w_q, sub_keys):
    n = x.shape[0]
    tb = min(RT_TOKENS, n)
    wqt = w_q.T.astype(jnp.bfloat16)
    keys = sub_keys.reshape(PEER_HEADS * 2, PEER_NKEYS, PEER_DKEY // 2).astype(jnp.bfloat16)
    return pl.pallas_call(
        _route_body,
        grid=(n // tb,),
        in_specs=[pl.BlockSpec((tb, D_MODEL), lambda i: (i, 0)),
                  pl.BlockSpec(wqt.shape, lambda i: (0, 0)),
                  pl.BlockSpec(keys.shape, lambda i: (0, 0, 0))],
        out_specs=[pl.BlockSpec((tb, NPAIR), lambda i: (i, 0)), pl.BlockSpec((tb, NPAIR), lambda i: (i, 0))],
        out_shape=[jax.ShapeDtypeStruct((n, NPAIR), jnp.int32), jax.ShapeDtypeStruct((n, NPAIR), F32)],
        compiler_params=pltpu.CompilerParams(dimension_semantics=("arbitrary",), vmem_limit_bytes=PEER_VMEM_LIMIT),
        name="peer_route",
    )(x, wqt, keys)


def peer_ffn(x, w_q, sub_keys, exp_u, exp_v):
    B, T, D = x.shape
    xf = x.reshape(B * T, D)
    idx4, gate = peer_route(xf, w_q, sub_keys)
    h = _peer_gather_call(_peer_u_body, "peer_u", idx4, [xf], pack_table(exp_u), NPAIR, [])
    out = _peer_gather_call(_peer_v_body, "peer_v", idx4, [h, gate], pack_table(exp_v), D,
                            [pltpu.VMEM((PEER_TB, NPAIR), F32)])
    return out.reshape(B, T, D)


def kernel(x, even_w_in, gla_a_w2, gla_a_b, gla_norm_g, rwkv_mu, rwkv_w0, rwkv_w2, rwkv_a0, rwkv_a2,
           rwkv_g2, rwkv_k_k, rwkv_k_a, rwkv_r_k, rwkv_lnx_g, rwkv_lnx_b, even_w_out, odd_w_in,
           odd_w_out, mix_ln_g, mix_ln_b, peer_w_q, peer_sub_keys, peer_u, peer_v, ffn_ln_g, ffn_ln_b):
    for layer in range(DEPTH):
        i = layer // 2
        if layer % 2 == 0:
            mix = even_mixer(x, even_w_in[i], gla_a_w2[i], gla_a_b[i], gla_norm_g[i], rwkv_mu[i], rwkv_w0[i],
                             rwkv_w2[i], rwkv_a0[i], rwkv_a2[i], rwkv_g2[i], rwkv_k_k[i], rwkv_k_a[i],
                             rwkv_r_k[i], rwkv_lnx_g[i], rwkv_lnx_b[i], even_w_out[i])
        else:
            mix = dsa_mixer(x, odd_w_in[i], odd_w_out[i])
        x = add_layer_norm(x, mix, mix_ln_g[layer], mix_ln_b[layer])
        ffn = peer_ffn(x, peer_w_q[layer], peer_sub_keys[layer], peer_u[layer], peer_v[layer])
        x = add_layer_norm(x, ffn, ffn_ln_g[layer], ffn_ln_b[layer])
    return x
```

```python
import functools

import jax
import jax.numpy as jnp
import numpy as np
from jax import lax
from jax.experimental import pallas as pl
from jax.experimental.pallas import tpu as pltpu

D_MODEL = 1024
BATCH = 32
SEQ = 2048
DEPTH = 4

GLA_HEADS = 4
GLA_VW = D_MODEL // 2
GLA_DV = GLA_VW // GLA_HEADS
GLA_KW = GLA_VW // 2
GLA_DK = GLA_KW // GLA_HEADS
GLA_LORA = 16
GLA_TAU = 16.0
GLA_CHUNK = 64
RWKV_W = D_MODEL - GLA_VW
RWKV_HEAD = 64
RWKV_HEADS = RWKV_W // RWKV_HEAD
RWKV_W_LORA = 32
RWKV_A_LORA = 32
RWKV_G_LORA = 96
RWKV_LNX_EPS = 64e-5
ATT_HEADS = 8
ATT_HEAD = D_MODEL // ATT_HEADS
IDX_HEADS = 8
IDX_DIM = 64
TOPK_MAX = 256
Q_BLOCK = 128
ROPE_THETA = 10000.0
NEG = -1e30
PEER_HEADS = 8
PEER_DKEY = 128
PEER_NKEYS = 128
PEER_N_EXPERTS = PEER_NKEYS * PEER_NKEYS
PEER_TOPK = 16
PEER_TOKEN_BLOCK = 128
DN_ALPHA = (2 * DEPTH) ** 0.25
DN_BETA = (8 * DEPTH) ** -0.25
LN_EPS = 1e-5
N_EVEN = (DEPTH + 1) // 2
N_ODD = DEPTH // 2

GLA_SPLITS = (GLA_KW, GLA_KW, GLA_VW, GLA_VW, GLA_LORA)
RWKV_SPLITS = (RWKV_W, RWKV_W, RWKV_W, RWKV_W_LORA, RWKV_A_LORA, RWKV_G_LORA)
GLA_IN = sum(GLA_SPLITS)
RWKV_IN = sum(RWKV_SPLITS)
EVEN_IN = GLA_IN + RWKV_IN
ODD_SPLITS = (ATT_HEADS * ATT_HEAD, ATT_HEAD, ATT_HEAD, IDX_HEADS * IDX_DIM, IDX_DIM, IDX_HEADS)
ODD_IN = sum(ODD_SPLITS)

F32 = jnp.float32


def _split(t, sizes):
    cuts = [int(c) for c in np.cumsum(sizes)[:-1]]
    return jnp.split(t, cuts, axis=-1)


LN_ROWS = 512


def _add_ln_body(x_ref, y_ref, g_ref, b_ref, o_ref):
    z = DN_ALPHA * x_ref[...] + y_ref[...]
    mu = jnp.mean(z, axis=-1, keepdims=True)
    zc = z - mu
    var = jnp.mean(zc * zc, axis=-1, keepdims=True)
    o_ref[...] = zc * lax.rsqrt(var + LN_EPS) * g_ref[...] + b_ref[...]


def add_layer_norm(x, y, g, b):
    B, T, D = x.shape
    n = B * T
    row = pl.BlockSpec((LN_ROWS, D), lambda i: (i, 0))
    vec = pl.BlockSpec((1, D), lambda i: (0, 0))
    out = pl.pallas_call(
        _add_ln_body,
        grid=(n // LN_ROWS,),
        in_specs=[row, row, vec, vec],
        out_specs=row,
        out_shape=jax.ShapeDtypeStruct((n, D), F32),
        name="add_layer_norm",
    )(x.reshape(n, D), y.reshape(n, D), g.reshape(1, D), b.reshape(1, D))
    return out.reshape(B, T, D)


def rope_tables(T, dim):
    inv = ROPE_THETA ** (-jnp.arange(0, dim, 2, dtype=F32) / dim)
    ang = jnp.arange(T, dtype=F32)[:, None] * inv[None, :]
    return jnp.cos(ang), jnp.sin(ang)


def apply_rope(x, cos, sin):
    x1, x2 = jnp.split(x, 2, axis=-1)
    return jnp.concatenate([x1 * cos - x2 * sin, x2 * cos + x1 * sin], -1).astype(x.dtype)


GLA_ROWS = 256


def _gla_body(q_ref, k_ref, v_ref, g_ref, al_ref, w2_ref, ab_ref, ng_ref, o_ref, st_ref):
    C = GLA_CHUNK
    bf = jnp.bfloat16

    @pl.when(pl.program_id(1) == 0)
    def _():
        st_ref[...] = jnp.zeros_like(st_ref)

    z = jnp.dot(al_ref[:, :GLA_LORA].astype(bf), w2_ref[...], preferred_element_type=F32) + ab_ref[...]
    la = -(jnp.maximum(-z, 0.0) + jnp.log1p(jnp.exp(-jnp.abs(z)))) / GLA_TAU
    row = lax.broadcasted_iota(jnp.int32, (C, GLA_KW), 0)
    tril = lax.broadcasted_iota(jnp.int32, (C, C), 0) >= lax.broadcasted_iota(jnp.int32, (C, C), 1)
    for n in range(q_ref.shape[0] // C):
        sl = slice(n * C, (n + 1) * C)
        b = la[sl]
        sh = 1
        while sh < C:
            b = b + jnp.where(row >= sh, pltpu.roll(b, sh, 0), 0.0)
            sh *= 2
        b_last = b[C - 1:C, :]
        k_c = k_ref[sl, :]
        q_in = q_ref[sl, :] * (GLA_DK ** -0.5) * jnp.exp(b)
        k_in = k_c * jnp.exp(-b)
        k_out = k_c * jnp.exp(b_last - b)
        dec = jnp.exp(b_last)
        for h in range(GLA_HEADS):
            ks = slice(h * GLA_DK, (h + 1) * GLA_DK)
            vs = slice(h * GLA_DV, (h + 1) * GLA_DV)
            qh = q_in[:, ks].astype(bf)
            vh = v_ref[sl, vs].astype(bf)
            a = lax.dot_general(qh, k_in[:, ks].astype(bf), (((1,), (1,)), ((), ())), preferred_element_type=F32)
            a = jnp.where(tril, a, 0.0)
            st = st_ref[h]
            o = jnp.dot(a.astype(bf), vh, preferred_element_type=F32)
            o = o + lax.dot_general(qh, st.astype(bf), (((1,), (1,)), ((), ())), preferred_element_type=F32)
            kv_t = lax.dot_general(vh, k_out[:, ks].astype(bf), (((0,), (0,)), ((), ())),
                                   preferred_element_type=F32)
            st_ref[h] = st * dec[:, ks] + kv_t
            o = o * lax.rsqrt(jnp.mean(o * o, axis=-1, keepdims=True) + LN_EPS) * ng_ref[...]
            gate = g_ref[sl, vs]
            o_ref[sl, vs] = o * (gate / (1.0 + jnp.exp(-gate)))


def gla_attention(p, a_w2, a_b, norm_g, B, T):
    R = min(GLA_ROWS, T)
    nr = T // R

    def cols(width, col0):
        return pl.BlockSpec((R, width), lambda b, i: (b * nr + i, col0 // width))

    def whole(shape):
        return pl.BlockSpec(shape, lambda b, i: (0,) * len(shape))

    return pl.pallas_call(
        _gla_body,
        grid=(B, nr),
        in_specs=[cols(GLA_KW, 0), cols(GLA_KW, GLA_KW), cols(GLA_VW, 2 * GLA_KW), cols(GLA_VW, 2 * GLA_KW + GLA_VW),
                  cols(LANES, 2 * GLA_KW + 2 * GLA_VW),
                  whole((GLA_LORA, GLA_KW)), whole((1, GLA_KW)), whole((1, GLA_DV))],
        out_specs=pl.BlockSpec((R, GLA_VW), lambda b, i: (b * nr + i, 0)),
        out_shape=jax.ShapeDtypeStruct((B * T, GLA_VW), F32),
        scratch_shapes=[pltpu.VMEM((GLA_HEADS, GLA_DV, GLA_DK), F32)],
        compiler_params=pltpu.CompilerParams(dimension_semantics=("arbitrary", "arbitrary"),
                                             vmem_limit_bytes=VMEM_LIMIT),
        name="gla_attention",
    )(p, p, p, p, p, a_w2.astype(jnp.bfloat16), a_b.reshape(1, GLA_KW), norm_g.reshape(1, GLA_DV))


SCAN_TC = 16


def _rwkv_scan_body(r_ref, w_ref, k_ref, v_ref, a_ref, b_ref, y_ref, s_ref):
    N = RWKV_HEAD

    @pl.when(pl.program_id(1) == 0)
    def _():
        s_ref[...] = jnp.zeros_like(s_ref)

    def step(tt, c):
        sa = jnp.zeros((N, LANES), F32)
        for j in range(N):
            sa = sa + s_ref[j] * a_ref[tt, j:j + 1, :]
        v_t = v_ref[tt]
        y = jnp.zeros((N, LANES), F32)
        for j in range(N):
            sj = s_ref[j] * w_ref[tt, j:j + 1, :] + sa * b_ref[tt, j:j + 1, :] + v_t * k_ref[tt, j:j + 1, :]
            s_ref[j] = sj
            y = y + sj * r_ref[tt, j:j + 1, :]
        y_ref[tt] = y
        return c

    lax.fori_loop(0, r_ref.shape[0], step, 0)


def rwkv7_scan(r, w, k, v, a, b):
    B, T, H, N = r.shape
    C = B * H
    tc = min(SCAN_TC, T)

    def to_scan(t):
        return t.astype(F32).transpose(1, 3, 0, 2).reshape(T, N, C)

    spec = pl.BlockSpec((tc, N, LANES), lambda g, i: (i, 0, g))
    y = pl.pallas_call(
        _rwkv_scan_body,
        grid=(C // LANES, T // tc),
        in_specs=[spec] * 6,
        out_specs=spec,
        out_shape=jax.ShapeDtypeStruct((T, N, C), F32),
        scratch_shapes=[pltpu.VMEM((N, N, LANES), F32)],
        compiler_params=pltpu.CompilerParams(dimension_semantics=("arbitrary", "arbitrary"),
                                             vmem_limit_bytes=VMEM_LIMIT),
        name="rwkv7_scan",
    )(*(to_scan(t) for t in (r, w, k, v, a, b)))
    return y.reshape(T, N, B, H).transpose(2, 0, 3, 1)


def even_mixer(x, w_in, gla_a_w2, gla_a_b, gla_norm_g, rwkv_mu, rwkv_w0, rwkv_w2, rwkv_a0, rwkv_a2,
               rwkv_g2, rwkv_k_k, rwkv_k_a, rwkv_r_k, rwkv_lnx_g, rwkv_lnx_b, w_out):
    B, T, _ = x.shape

    def heads(t, h):
        return t.reshape(B, T, h, -1)

    p = matmul(x.reshape(B * T, -1), w_in).reshape(B, T, -1)
    gla_p, rw_p = p[..., :GLA_IN], p[..., GLA_IN:]
    gla_out = gla_attention(p.reshape(B * T, -1), gla_a_w2, gla_a_b, gla_norm_g, B, T).reshape(B, T, GLA_VW)
    prev = jnp.pad(rw_p, ((0, 0), (1, 0), (0, 0)))[:, :-1]
    rw_p = rw_p + (prev - rw_p) * rwkv_mu
    r, k, v, wl, al, gl = _split(rw_p, RWKV_SPLITS)
    w_log = -jax.nn.softplus(-(rwkv_w0 + jnp.tanh(wl) @ rwkv_w2).astype(F32)) - 0.5
    decay = jnp.exp(-jnp.exp(w_log))
    a = jax.nn.sigmoid((rwkv_a0 + al @ rwkv_a2).astype(F32))
    g = jax.nn.sigmoid(gl) @ rwkv_g2
    kk = heads((k * rwkv_k_k).astype(F32), RWKV_HEADS)
    kk = kk / jnp.maximum(jnp.sqrt(jnp.sum(jnp.square(kk), -1, keepdims=True)), 1e-12)
    k = k * (1.0 + (a - 1.0) * rwkv_k_a)
    a_h = heads(a, RWKV_HEADS)
    r_h, k_h, v_h = heads(r, RWKV_HEADS), heads(k, RWKV_HEADS), heads(v, RWKV_HEADS)
    y = rwkv7_scan(r_h, heads(decay, RWKV_HEADS), k_h, v_h, -kk, kk * a_h)
    mu = y.mean(-1, keepdims=True)
    var = jnp.square(y - mu).mean(-1, keepdims=True)
    y = (y - mu) * lax.rsqrt(var + RWKV_LNX_EPS) * rwkv_lnx_g.reshape(RWKV_HEADS, RWKV_HEAD) \
        + rwkv_lnx_b.reshape(RWKV_HEADS, RWKV_HEAD)
    bonus = jnp.sum((r_h * k_h * rwkv_r_k).astype(F32), -1, keepdims=True) * v_h.astype(F32)
    rwkv_out = ((y + bonus).reshape(B, T, RWKV_W) * g.astype(F32)).astype(x.dtype)
    return matmul(jnp.concatenate([gla_out, rwkv_out], -1).reshape(B * T, -1), w_out).reshape(B, T, -1)


MM_ROWS = 512
VMEM_LIMIT = 48 * 1024 * 1024


def _mm_body(x_ref, w_ref, o_ref):
    o_ref[...] = jnp.dot(x_ref[...].astype(jnp.bfloat16), w_ref[...], preferred_element_type=F32)


def matmul(x, w):
    M, K = x.shape
    N = w.shape[1]
    tm = min(MM_ROWS, M)
    return pl.pallas_call(
        _mm_body,
        grid=(M // tm,),
        in_specs=[pl.BlockSpec((tm, K), lambda i: (i, 0)), pl.BlockSpec((K, N), lambda i: (0, 0))],
        out_specs=pl.BlockSpec((tm, N), lambda i: (i, 0)),
        out_shape=jax.ShapeDtypeStruct((M, N), F32),
        compiler_params=pltpu.CompilerParams(vmem_limit_bytes=VMEM_LIMIT),
        name="matmul",
    )(x, w.astype(jnp.bfloat16))


LANES = 128
INT_MIN = -2 ** 31
DSA_Q0 = 0
DSA_QI0 = ATT_HEADS * ATT_HEAD
DSA_K0 = DSA_QI0 + IDX_HEADS * IDX_DIM
DSA_V0 = DSA_K0 + ATT_HEAD
DSA_KIW0 = DSA_V0 + ATT_HEAD
DSA_COLS = DSA_KIW0 + LANES
DSA_KEY_CLASSES = 4


def _swap32(x):
    lane = lax.broadcasted_iota(jnp.int32, x.shape, 1)
    return jnp.where((lane % IDX_DIM) < IDX_DIM // 2, pltpu.roll(x, LANES - IDX_DIM // 2, 1),
                     pltpu.roll(x, IDX_DIM // 2, 1))


def _count_ge(key, cand):
    return jnp.sum(jnp.where(key >= cand, 1.0, 0.0), axis=1, keepdims=True)


def _dsa_body(n_sel, q_ref, qi_ref, kiwq_ref, k_ref, v_ref, kiw_ref, cqb_ref, sqb_ref, cib_ref, sib_ref,
              cq_ref, sq_ref, ci_ref, si_ref, o_ref, krot_ref, vbf_ref, kirot_ref):
    j = pl.program_id(1)
    T = k_ref.shape[0]
    QB = q_ref.shape[0]

    @pl.when(j == 0)
    def _():
        k = k_ref[...]
        krot_ref[...] = (k * cq_ref[...] + pltpu.roll(k, ATT_HEAD // 2, 1) * sq_ref[...]).astype(jnp.bfloat16)
        vbf_ref[...] = v_ref[...].astype(jnp.bfloat16)
        kiw = kiw_ref[...]
        kirot_ref[...] = (kiw * ci_ref[...] + _swap32(kiw) * si_ref[...]).astype(jnp.bfloat16)

    def attend(nk):
        col_bits = (nk - 1).bit_length()
        ci_b = cib_ref[...]
        si_b = sib_ref[...]
        ki = kirot_ref[:nk, :IDX_DIM]
        wi = kiwq_ref[:, IDX_DIM:IDX_DIM + IDX_HEADS] * (IDX_HEADS ** -0.5 * IDX_DIM ** -0.5)
        score = jnp.zeros((QB, nk), F32)
        for g in range(IDX_HEADS * IDX_DIM // LANES):
            qg = qi_ref[:, g * LANES:(g + 1) * LANES]
            qg = (qg * ci_b + _swap32(qg) * si_b).astype(jnp.bfloat16)
            for hh in range(LANES // IDX_DIM):
                h = g * (LANES // IDX_DIM) + hh
                s = lax.dot_general(qg[:, hh * IDX_DIM:(hh + 1) * IDX_DIM], ki, (((1,), (1,)), ((), ())),
                                    preferred_element_type=F32)
                score = score + wi[:, h:h + 1] * jnp.maximum(s, 0.0)
        col = lax.broadcasted_iota(jnp.int32, (QB, nk), 1)
        row = j * QB + lax.broadcasted_iota(jnp.int32, (QB, nk), 0)
        causal = col <= row
        score = jnp.where(causal, score, NEG)

        bits = pltpu.bitcast(score, jnp.int32)
        key = jnp.where(bits < 0, bits ^ jnp.int32(0x7FFFFFFF), bits)
        kf = float(n_sel)
        t0 = jnp.where(_count_ge(key, jnp.zeros((QB, 1), jnp.int32)) >= kf, jnp.int32(0), jnp.int32(INT_MIN))

        def bit_step(i, t):
            cand = t | (jnp.int32(1) << (30 - i))
            return jnp.where(_count_ge(key, cand) >= kf, cand, t)

        thr = lax.fori_loop(0, 31, bit_step, t0)
        gt = key > thr
        eq = key == thr
        need = kf - jnp.sum(jnp.where(gt, 1.0, 0.0), axis=1, keepdims=True)

        def tie_step(i, jj):
            cand = jj | (jnp.int32(1) << (col_bits - 1 - i))
            c = jnp.sum(jnp.where(eq & (col < cand), 1.0, 0.0), axis=1, keepdims=True)
            return jnp.where(c < need, cand, jj)

        jstar = lax.fori_loop(0, col_bits, tie_step, jnp.zeros((QB, 1), jnp.int32))
        sel = (gt | (eq & (col <= jstar))) & causal
        bias = jnp.where(sel, 0.0, NEG)

        cq_b = cqb_ref[...]
        sq_b = sqb_ref[...]
        kr = krot_ref[:nk, :]
        vb = vbf_ref[:nk, :]
        for h in range(ATT_HEADS):
            qh = q_ref[:, h * ATT_HEAD:(h + 1) * ATT_HEAD]
            qh = (qh * cq_b + pltpu.roll(qh, ATT_HEAD // 2, 1) * sq_b).astype(jnp.bfloat16)
            logit = lax.dot_general(qh, kr, (((1,), (1,)), ((), ())), preferred_element_type=F32)
            logit = logit * (ATT_HEAD ** -0.5) + bias
            m = jnp.max(logit, axis=1, keepdims=True)
            p = jnp.exp(logit - m)
            l = jnp.sum(p, axis=1, keepdims=True)
            o = jnp.dot(p.astype(jnp.bfloat16), vb, preferred_element_type=F32)
            o_ref[:, h * ATT_HEAD:(h + 1) * ATT_HEAD] = o / l

    nq = T // QB
    n_cls = DSA_KEY_CLASSES if nq % DSA_KEY_CLASSES == 0 else 1
    per = nq // n_cls
    for c in range(n_cls):
        nk = max((c + 1) * per * QB, n_sel)
        pl.when((j >= c * per) & (j < (c + 1) * per))(functools.partial(attend, nk))


def dsa_attention(p, B, T):
    n_sel = min(TOPK_MAX, T // 4)
    nq = T // Q_BLOCK
    cos_a, sin_a = rope_tables(T, ATT_HEAD)
    cos_i, sin_i = rope_tables(T, IDX_DIM)
    cq = jnp.concatenate([cos_a, cos_a], -1)
    sq = jnp.concatenate([-sin_a, sin_a], -1)
    ci = jnp.concatenate([cos_i, cos_i, cos_i, cos_i], -1)
    si = jnp.concatenate([-sin_i, sin_i, -sin_i, sin_i], -1)

    def qrow(width, col0):
        return pl.BlockSpec((Q_BLOCK, width), lambda b, j: (b * nq + j, col0 // width))

    def krow(col0):
        return pl.BlockSpec((T, LANES), lambda b, j: (b, col0 // LANES))

    tab_q = pl.BlockSpec((Q_BLOCK, LANES), lambda b, j: (j, 0))
    tab_k = pl.BlockSpec((T, LANES), lambda b, j: (0, 0))
    return pl.pallas_call(
        functools.partial(_dsa_body, n_sel),
        grid=(B, nq),
        in_specs=[qrow(ATT_HEADS * ATT_HEAD, DSA_Q0), qrow(IDX_HEADS * IDX_DIM, DSA_QI0), qrow(LANES, DSA_KIW0),
                  krow(DSA_K0), krow(DSA_V0), krow(DSA_KIW0),
                  tab_q, tab_q, tab_q, tab_q, tab_k, tab_k, tab_k, tab_k],
        out_specs=pl.BlockSpec((Q_BLOCK, D_MODEL), lambda b, j: (b * nq + j, 0)),
        out_shape=jax.ShapeDtypeStruct((B * T, D_MODEL), F32),
        scratch_shapes=[pltpu.VMEM((T, ATT_HEAD), jnp.bfloat16), pltpu.VMEM((T, ATT_HEAD), jnp.bfloat16),
                        pltpu.VMEM((T, LANES), jnp.bfloat16)],
        compiler_params=pltpu.CompilerParams(dimension_semantics=("arbitrary", "arbitrary"),
                                             vmem_limit_bytes=VMEM_LIMIT),
        name="dsa_attention",
    )(p, p, p, p, p, p, cq, sq, ci, si, cq, sq, ci, si)


def dsa_mixer(x, w_in, w_out):
    B, T, D = x.shape
    wq, wk, wv, wqi, wki, wwi = _split(w_in, ODD_SPLITS)
    pad = jnp.zeros((D, DSA_COLS - DSA_KIW0 - IDX_DIM - IDX_HEADS), w_in.dtype)
    w_cat = jnp.concatenate([wq, wqi, wk, wv, wki, wwi, pad], axis=1)
    p = matmul(x.reshape(B * T, D), w_cat)
    o = dsa_attention(p, B, T)
    return matmul(o, w_out).reshape(B, T, D)


NPAIR = PEER_HEADS * PEER_TOPK
PEER_TB = 128
SLAB = 4
GS = 136
PAIRS_PER_ITER = 8
RT_TOKENS = 256
PEER_VMEM_LIMIT = 56 * 1024 * 1024
NEG_INF = float("-inf")


def pack_table(tab):
    E = tab.shape[0]
    b = lax.bitcast_convert_type(tab.astype(jnp.bfloat16), jnp.uint16).astype(jnp.uint32)
    b = b.reshape(E, SLAB, 2, LANES)
    w = b[:, :, 0, :] | (b[:, :, 1, :] << 16)
    return lax.bitcast_convert_type(w, jnp.int32).reshape(E * SLAB, LANES)


def _gather_rows(idx_ref, base, tab_ref, gt_ref):
    for p in range(NPAIR):
        i = pl.multiple_of(idx_ref[0, 0, base + p], SLAB)
        gt_ref[pl.ds(p, SLAB, stride=GS), :] = tab_ref[pl.ds(i, SLAB), :]


def _unpacked(gt_ref):
    parts = []
    for j in range(SLAB):
        w = gt_ref[pl.ds(j * GS, NPAIR), :]
        parts.append(pltpu.bitcast(w << 16, F32).astype(jnp.bfloat16))
        parts.append(pltpu.bitcast(w & jnp.int32(-65536), F32).astype(jnp.bfloat16))
    return jnp.concatenate(parts, axis=1)


def _token_loop(idx_ref, tab_ref, gta_ref, gtb_ref, compute):
    _gather_rows(idx_ref, 0, tab_ref, gta_ref)

    def pairs(i, c):
        for k in range(PAIRS_PER_ITER):
            t = 2 * (i * PAIRS_PER_ITER + k)
            _gather_rows(idx_ref, (t + 1) * NPAIR, tab_ref, gtb_ref)
            compute(t, _unpacked(gta_ref))
            _gather_rows(idx_ref, jnp.minimum(t + 2, PEER_TB - 1) * NPAIR, tab_ref, gta_ref)
            compute(t + 1, _unpacked(gtb_ref))
        return c

    lax.fori_loop(0, PEER_TB // (2 * PAIRS_PER_ITER), pairs, 0)


def _peer_u_body(idx_ref, x_ref, tab_ref, h_ref, gta_ref, gtb_ref):
    def compute(t, g):
        xt = x_ref[pl.ds(t, 1), :].astype(jnp.bfloat16)
        h_ref[pl.ds(t, 1), :] = lax.dot_general(xt, g, (((1,), (1,)), ((), ())), preferred_element_type=F32)

    _token_loop(idx_ref, tab_ref, gta_ref, gtb_ref, compute)


def _peer_v_body(idx_ref, h_ref, gate_ref, tab_ref, o_ref, coef_ref, gta_ref, gtb_ref):
    hv = h_ref[...]
    coef_ref[...] = gate_ref[...] * (0.5 * hv * (1.0 + lax.erf(hv * (2.0 ** -0.5))))

    def compute(t, g):
        ct = coef_ref[pl.ds(t, 1), :].astype(jnp.bfloat16)
        o_ref[pl.ds(t, 1), :] = jnp.dot(ct, g, preferred_element_type=F32)

    _token_loop(idx_ref, tab_ref, gta_ref, gtb_ref, compute)


def _peer_gather_call(body, name, idx4, acts, tab, out_w, extra_scratch):
    n = acts[0].shape[0]
    nb = n // PEER_TB
    gather_tile = pltpu.VMEM((SLAB * GS, LANES), jnp.int32)
    return pl.pallas_call(
        body,
        grid=(nb,),
        in_specs=[pl.BlockSpec((1, 1, PEER_TB * NPAIR), lambda i: (i, 0, 0), memory_space=pltpu.SMEM)]
        + [pl.BlockSpec((PEER_TB, a.shape[1]), lambda i: (i, 0)) for a in acts]
        + [pl.BlockSpec(memory_space=pltpu.VMEM)],
        out_specs=pl.BlockSpec((PEER_TB, out_w), lambda i: (i, 0)),
        out_shape=jax.ShapeDtypeStruct((n, out_w), F32),
        scratch_shapes=extra_scratch + [gather_tile, gather_tile],
        compiler_params=pltpu.CompilerParams(dimension_semantics=("arbitrary",),
                                             vmem_limit_bytes=PEER_VMEM_LIMIT),
        name=name,
    )(idx4.reshape(nb, 1, PEER_TB * NPAIR), *acts, tab)


def _top16(vals, ids):
    n, T = vals.shape
    pos = lax.broadcasted_iota(jnp.int32, (n, T), 0)
    slot = lax.broadcasted_iota(jnp.int32, (PEER_TOPK, T), 0)
    top_v = jnp.zeros((PEER_TOPK, T), F32)
    top_i = jnp.zeros((PEER_TOPK, T), jnp.int32)
    for r in range(PEER_TOPK):
        m = jnp.max(vals, axis=0, keepdims=True)
        am = jnp.min(jnp.where(vals == m, pos, n), axis=0, keepdims=True)
        hit = pos == am
        sel = am if ids is None else jnp.max(jnp.where(hit, ids, -1), axis=0, keepdims=True)
        top_v = jnp.where(slot == r, m, top_v)
        top_i = jnp.where(slot == r, sel, top_i)
        vals = jnp.where(hit, NEG_INF, vals)
    return top_v, top_i


def _route_body(x_ref, wqt_ref, keys_ref, idx_ref, gate_ref):
    T = x_ref.shape[0]
    qt = lax.dot_general(wqt_ref[...], x_ref[...].astype(jnp.bfloat16), (((1,), (1,)), ((), ())),
                         preferred_element_type=F32)
    half = PEER_DKEY // 2
    idx_rows, gate_rows = [], []
    for h in range(PEER_HEADS):
        tops = []
        for c in range(2):
            g = h * 2 + c
            s = jnp.dot(keys_ref[g], qt[g * half:(g + 1) * half, :].astype(jnp.bfloat16),
                        preferred_element_type=F32)
            tops.append(_top16(s, None))
        (s0, i0), (s1, i1) = tops
        sub8 = lax.broadcasted_iota(jnp.int32, (8, T), 0)
        cs, cids = [s0[0:1, :] + s1], [i0[0:1, :] * PEER_NKEYS + i1]
        for a in range(1, 8):
            keep = sub8 < PEER_TOPK // (a + 1)
            cs.append(jnp.where(keep, s0[a:a + 1, :] + s1[0:8, :], NEG_INF))
            cids.append(i0[a:a + 1, :] * PEER_NKEYS + i1[0:8, :])
        cs.append(s0[8:16, :] + s1[0:1, :])
        cids.append(i0[8:16, :] * PEER_NKEYS + i1[0:1, :])
        best_s, best_i = _top16(jnp.concatenate(cs, axis=0), jnp.concatenate(cids, axis=0))
        e = jnp.exp(best_s - best_s[0:1, :])
        gate_rows.append(e / jnp.sum(e, axis=0, keepdims=True))
        idx_rows.append(best_i * SLAB)
    idx_ref[...] = jnp.concatenate(idx_rows, axis=0).T
    gate_ref[...] = jnp.concatenate(gate_rows, axis=0).T


def peer_route(x, w_q, sub_keys):
    n = x.shape[0]
    tb = min(RT_TOKENS, n)
    wqt = w_q.T.astype(jnp.bfloat16)
    keys = sub_keys.reshape(PEER_HEADS * 2, PEER_NKEYS, PEER_DKEY // 2).astype(jnp.bfloat16)
    return pl.pallas_call(
        _route_body,
        grid=(n // tb,),
        in_specs=[pl.BlockSpec((tb, D_MODEL), lambda i: (i, 0)),
                  pl.BlockSpec(wqt.shape, lambda i: (0, 0)),
                  pl.BlockSpec(keys.shape, lambda i: (0, 0, 0))],
        out_specs=[pl.BlockSpec((tb, NPAIR), lambda i: (i, 0)), pl.BlockSpec((tb, NPAIR), lambda i: (i, 0))],
        out_shape=[jax.ShapeDtypeStruct((n, NPAIR), jnp.int32), jax.ShapeDtypeStruct((n, NPAIR), F32)],
        compiler_params=pltpu.CompilerParams(dimension_semantics=("arbitrary",), vmem_limit_bytes=PEER_VMEM_LIMIT),
        name="peer_route",
    )(x, wqt, keys)


def peer_ffn(x, w_q, sub_keys, exp_u, exp_v):
    B, T, D = x.shape
    xf = x.reshape(B * T, D)
    idx4, gate = peer_route(xf, w_q, sub_keys)
    h = _peer_gather_call(_peer_u_body, "peer_u", idx4, [xf], pack_table(exp_u), NPAIR, [])
    out = _peer_gather_call(_peer_v_body, "peer_v", idx4, [h, gate], pack_table(exp_v), D,
                            [pltpu.VMEM((PEER_TB, NPAIR), F32)])
    return out.reshape(B, T, D)


def kernel(x, even_w_in, gla_a_w2, gla_a_b, gla_norm_g, rwkv_mu, rwkv_w0, rwkv_w2, rwkv_a0, rwkv_a2,
           rwkv_g2, rwkv_k_k, rwkv_k_a, rwkv_r_k, rwkv_lnx_g, rwkv_lnx_b, even_w_out, odd_w_in,
           odd_w_out, mix_ln_g, mix_ln_b, peer_w_q, peer_sub_keys, peer_u, peer_v, ffn_ln_g, ffn_ln_b):
    for layer in range(DEPTH):
        i = layer // 2
        if layer % 2 == 0:
            mix = even_mixer(x, even_w_in[i], gla_a_w2[i], gla_a_b[i], gla_norm_g[i], rwkv_mu[i], rwkv_w0[i],
                             rwkv_w2[i], rwkv_a0[i], rwkv_a2[i], rwkv_g2[i], rwkv_k_k[i], rwkv_k_a[i],
                             rwkv_r_k[i], rwkv_lnx_g[i], rwkv_lnx_b[i], even_w_out[i])
        else:
            mix = dsa_mixer(x, odd_w_in[i], odd_w_out[i])
        x = add_layer_norm(x, mix, mix_ln_g[layer], mix_ln_b[layer])
        ffn = peer_ffn(x, peer_w_q[layer], peer_sub_keys[layer], peer_u[layer], peer_v[layer])
        x = add_layer_norm(x, ffn, ffn_ln_g[layer], ffn_ln_b[layer])
    return x
```

```python
import functools

import jax
import jax.numpy as jnp
import numpy as np
from jax import lax
from jax.experimental import pallas as pl
from jax.experimental.pallas import tpu as pltpu

D_MODEL = 1024
BATCH = 32
SEQ = 2048
DEPTH = 4

GLA_HEADS = 4
GLA_VW = D_MODEL // 2
GLA_DV = GLA_VW // GLA_HEADS
GLA_KW = GLA_VW // 2
GLA_DK = GLA_KW // GLA_HEADS
GLA_LORA = 16
GLA_TAU = 16.0
GLA_CHUNK = 64
RWKV_W = D_MODEL - GLA_VW
RWKV_HEAD = 64
RWKV_HEADS = RWKV_W // RWKV_HEAD
RWKV_W_LORA = 32
RWKV_A_LORA = 32
RWKV_G_LORA = 96
RWKV_LNX_EPS = 64e-5
ATT_HEADS = 8
ATT_HEAD = D_MODEL // ATT_HEADS
IDX_HEADS = 8
IDX_DIM = 64
TOPK_MAX = 256
Q_BLOCK = 128
ROPE_THETA = 10000.0
NEG = -1e30
PEER_HEADS = 8
PEER_DKEY = 128
PEER_NKEYS = 128
PEER_N_EXPERTS = PEER_NKEYS * PEER_NKEYS
PEER_TOPK = 16
PEER_TOKEN_BLOCK = 128
DN_ALPHA = (2 * DEPTH) ** 0.25
DN_BETA = (8 * DEPTH) ** -0.25
LN_EPS = 1e-5
N_EVEN = (DEPTH + 1) // 2
N_ODD = DEPTH // 2

GLA_SPLITS = (GLA_KW, GLA_KW, GLA_VW, GLA_VW, GLA_LORA)
RWKV_SPLITS = (RWKV_W, RWKV_W, RWKV_W, RWKV_W_LORA, RWKV_A_LORA, RWKV_G_LORA)
GLA_IN = sum(GLA_SPLITS)
RWKV_IN = sum(RWKV_SPLITS)
EVEN_IN = GLA_IN + RWKV_IN
ODD_SPLITS = (ATT_HEADS * ATT_HEAD, ATT_HEAD, ATT_HEAD, IDX_HEADS * IDX_DIM, IDX_DIM, IDX_HEADS)
ODD_IN = sum(ODD_SPLITS)

F32 = jnp.float32


def _split(t, sizes):
    cuts = [int(c) for c in np.cumsum(sizes)[:-1]]
    return jnp.split(t, cuts, axis=-1)


LN_ROWS = 512


def _add_ln_body(x_ref, y_ref, g_ref, b_ref, o_ref):
    z = DN_ALPHA * x_ref[...] + y_ref[...]
    mu = jnp.mean(z, axis=-1, keepdims=True)
    zc = z - mu
    var = jnp.mean(zc * zc, axis=-1, keepdims=True)
    o_ref[...] = zc * lax.rsqrt(var + LN_EPS) * g_ref[...] + b_ref[...]


def add_layer_norm(x, y, g, b):
    B, T, D = x.shape
    n = B * T
    row = pl.BlockSpec((LN_ROWS, D), lambda i: (i, 0))
    vec = pl.BlockSpec((1, D), lambda i: (0, 0))
    out = pl.pallas_call(
        _add_ln_body,
        grid=(n // LN_ROWS,),
        in_specs=[row, row, vec, vec],
        out_specs=row,
        out_shape=jax.ShapeDtypeStruct((n, D), F32),
        name="add_layer_norm",
    )(x.reshape(n, D), y.reshape(n, D), g.reshape(1, D), b.reshape(1, D))
    return out.reshape(B, T, D)


def rope_tables(T, dim):
    inv = ROPE_THETA ** (-jnp.arange(0, dim, 2, dtype=F32) / dim)
    ang = jnp.arange(T, dtype=F32)[:, None] * inv[None, :]
    return jnp.cos(ang), jnp.sin(ang)


def apply_rope(x, cos, sin):
    x1, x2 = jnp.split(x, 2, axis=-1)
    return jnp.concatenate([x1 * cos - x2 * sin, x2 * cos + x1 * sin], -1).astype(x.dtype)


GLA_ROWS = 256


def _gla_body(q_ref, k_ref, v_ref, g_ref, al_ref, w2_ref, ab_ref, ng_ref, o_ref, st_ref):
    C = GLA_CHUNK
    bf = jnp.bfloat16

    @pl.when(pl.program_id(1) == 0)
    def _():
        st_ref[...] = jnp.zeros_like(st_ref)

    z = jnp.dot(al_ref[:, :GLA_LORA].astype(bf), w2_ref[...], preferred_element_type=F32) + ab_ref[...]
    la = -(jnp.maximum(-z, 0.0) + jnp.log1p(jnp.exp(-jnp.abs(z)))) / GLA_TAU
    row = lax.broadcasted_iota(jnp.int32, (C, GLA_KW), 0)
    tril = lax.broadcasted_iota(jnp.int32, (C, C), 0) >= lax.broadcasted_iota(jnp.int32, (C, C), 1)
    for n in range(q_ref.shape[0] // C):
        sl = slice(n * C, (n + 1) * C)
        b = la[sl]
        sh = 1
        while sh < C:
            b = b + jnp.where(row >= sh, pltpu.roll(b, sh, 0), 0.0)
            sh *= 2
        b_last = b[C - 1:C, :]
        k_c = k_ref[sl, :]
        q_in = q_ref[sl, :] * (GLA_DK ** -0.5) * jnp.exp(b)
        k_in = k_c * jnp.exp(-b)
        k_out = k_c * jnp.exp(b_last - b)
        dec = jnp.exp(b_last)
        for h in range(GLA_HEADS):
            ks = slice(h * GLA_DK, (h + 1) * GLA_DK)
            vs = slice(h * GLA_DV, (h + 1) * GLA_DV)
            qh = q_in[:, ks].astype(bf)
            vh = v_ref[sl, vs].astype(bf)
            a = lax.dot_general(qh, k_in[:, ks].astype(bf), (((1,), (1,)), ((), ())), preferred_element_type=F32)
            a = jnp.where(tril, a, 0.0)
            st = st_ref[h]
            o = jnp.dot(a.astype(bf), vh, preferred_element_type=F32)
            o = o + lax.dot_general(qh, st.astype(bf), (((1,), (1,)), ((), ())), preferred_element_type=F32)
            kv_t = lax.dot_general(vh, k_out[:, ks].astype(bf), (((0,), (0,)), ((), ())),
                                   preferred_element_type=F32)
            st_ref[h] = st * dec[:, ks] + kv_t
            o = o * lax.rsqrt(jnp.mean(o * o, axis=-1, keepdims=True) + LN_EPS) * ng_ref[...]
            gate = g_ref[sl, vs]
            o_ref[sl, vs] = o * (gate / (1.0 + jnp.exp(-gate)))


def gla_attention(p, a_w2, a_b, norm_g, B, T):
    R = min(GLA_ROWS, T)
    nr = T // R

    def cols(width, col0):
        return pl.BlockSpec((R, width), lambda b, i: (b * nr + i, col0 // width))

    def whole(shape):
        return pl.BlockSpec(shape, lambda b, i: (0,) * len(shape))

    return pl.pallas_call(
        _gla_body,
        grid=(B, nr),
        in_specs=[cols(GLA_KW, 0), cols(GLA_KW, GLA_KW), cols(GLA_VW, 2 * GLA_KW), cols(GLA_VW, 2 * GLA_KW + GLA_VW),
                  cols(LANES, 2 * GLA_KW + 2 * GLA_VW),
                  whole((GLA_LORA, GLA_KW)), whole((1, GLA_KW)), whole((1, GLA_DV))],
        out_specs=pl.BlockSpec((R, GLA_VW), lambda b, i: (b * nr + i, 0)),
        out_shape=jax.ShapeDtypeStruct((B * T, GLA_VW), F32),
        scratch_shapes=[pltpu.VMEM((GLA_HEADS, GLA_DV, GLA_DK), F32)],
        compiler_params=pltpu.CompilerParams(dimension_semantics=("arbitrary", "arbitrary"),
                                             vmem_limit_bytes=VMEM_LIMIT),
        name="gla_attention",
    )(p, p, p, p, p, a_w2.astype(jnp.bfloat16), a_b.reshape(1, GLA_KW), norm_g.reshape(1, GLA_DV))


SCAN_TC = 16


def _rwkv_scan_body(r_ref, w_ref, k_ref, v_ref, a_ref, b_ref, y_ref, s_ref):
    N = RWKV_HEAD

    @pl.when(pl.program_id(1) == 0)
    def _():
        s_ref[...] = jnp.zeros_like(s_ref)

    def step(tt, c):
        sa = jnp.zeros((N, LANES), F32)
        for j in range(N):
            sa = sa + s_ref[j] * a_ref[tt, j:j + 1, :]
        v_t = v_ref[tt]
        y = jnp.zeros((N, LANES), F32)
        for j in range(N):
            sj = s_ref[j] * w_ref[tt, j:j + 1, :] + sa * b_ref[tt, j:j + 1, :] + v_t * k_ref[tt, j:j + 1, :]
            s_ref[j] = sj
            y = y + sj * r_ref[tt, j:j + 1, :]
        y_ref[tt] = y
        return c

    lax.fori_loop(0, r_ref.shape[0], step, 0)


def rwkv7_scan(r, w, k, v, a, b):
    B, T, H, N = r.shape
    C = B * H
    tc = min(SCAN_TC, T)

    def to_scan(t):
        return t.astype(F32).transpose(1, 3, 0, 2).reshape(T, N, C)

    spec = pl.BlockSpec((tc, N, LANES), lambda g, i: (i, 0, g))
    y = pl.pallas_call(
        _rwkv_scan_body,
        grid=(C // LANES, T // tc),
        in_specs=[spec] * 6,
        out_specs=spec,
        out_shape=jax.ShapeDtypeStruct((T, N, C), F32),
        scratch_shapes=[pltpu.VMEM((N, N, LANES), F32)],
        compiler_params=pltpu.CompilerParams(dimension_semantics=("arbitrary", "arbitrary"),
                                             vmem_limit_bytes=VMEM_LIMIT),
        name="rwkv7_scan",
    )(*(to_scan(t) for t in (r, w, k, v, a, b)))
    return y.reshape(T, N, B, H).transpose(2, 0, 3, 1)


def even_mixer(x, w_in, gla_a_w2, gla_a_b, gla_norm_g, rwkv_mu, rwkv_w0, rwkv_w2, rwkv_a0, rwkv_a2,
               rwkv_g2, rwkv_k_k, rwkv_k_a, rwkv_r_k, rwkv_lnx_g, rwkv_lnx_b, w_out):
    B, T, _ = x.shape

    def heads(t, h):
        return t.reshape(B, T, h, -1)

    p = matmul(x.reshape(B * T, -1), w_in).reshape(B, T, -1)
    gla_p, rw_p = p[..., :GLA_IN], p[..., GLA_IN:]
    gla_out = gla_attention(p.reshape(B * T, -1), gla_a_w2, gla_a_b, gla_norm_g, B, T).reshape(B, T, GLA_VW)
    prev = jnp.pad(rw_p, ((0, 0), (1, 0), (0, 0)))[:, :-1]
    rw_p = rw_p + (prev - rw_p) * rwkv_mu
    r, k, v, wl, al, gl = _split(rw_p, RWKV_SPLITS)
    w_log = -jax.nn.softplus(-(rwkv_w0 + jnp.tanh(wl) @ rwkv_w2).astype(F32)) - 0.5
    decay = jnp.exp(-jnp.exp(w_log))
    a = jax.nn.sigmoid((rwkv_a0 + al @ rwkv_a2).astype(F32))
    g = jax.nn.sigmoid(gl) @ rwkv_g2
    kk = heads((k * rwkv_k_k).astype(F32), RWKV_HEADS)
    kk = kk / jnp.maximum(jnp.sqrt(jnp.sum(jnp.square(kk), -1, keepdims=True)), 1e-12)
    k = k * (1.0 + (a - 1.0) * rwkv_k_a)
    a_h = heads(a, RWKV_HEADS)
    r_h, k_h, v_h = heads(r, RWKV_HEADS), heads(k, RWKV_HEADS), heads(v, RWKV_HEADS)
    y = rwkv7_scan(r_h, heads(decay, RWKV_HEADS), k_h, v_h, -kk, kk * a_h)
    mu = y.mean(-1, keepdims=True)
    var = jnp.square(y - mu).mean(-1, keepdims=True)
    y = (y - mu) * lax.rsqrt(var + RWKV_LNX_EPS) * rwkv_lnx_g.reshape(RWKV_HEADS, RWKV_HEAD) \
        + rwkv_lnx_b.reshape(RWKV_HEADS, RWKV_HEAD)
    bonus = jnp.sum((r_h * k_h * rwkv_r_k).astype(F32), -1, keepdims=True) * v_h.astype(F32)
    rwkv_out = ((y + bonus).reshape(B, T, RWKV_W) * g.astype(F32)).astype(x.dtype)
    return matmul(jnp.concatenate([gla_out, rwkv_out], -1).reshape(B * T, -1), w_out).reshape(B, T, -1)


MM_ROWS = 512
VMEM_LIMIT = 48 * 1024 * 1024


def _mm_body(x_ref, w_ref, o_ref):
    o_ref[...] = jnp.dot(x_ref[...].astype(jnp.bfloat16), w_ref[...], preferred_element_type=F32)


def matmul(x, w):
    M, K = x.shape
    N = w.shape[1]
    tm = min(MM_ROWS, M)
    return pl.pallas_call(
        _mm_body,
        grid=(M // tm,),
        in_specs=[pl.BlockSpec((tm, K), lambda i: (i, 0)), pl.BlockSpec((K, N), lambda i: (0, 0))],
        out_specs=pl.BlockSpec((tm, N), lambda i: (i, 0)),
        out_shape=jax.ShapeDtypeStruct((M, N), F32),
        compiler_params=pltpu.CompilerParams(vmem_limit_bytes=VMEM_LIMIT),
        name="matmul",
    )(x, w.astype(jnp.bfloat16))


LANES = 128
INT_MIN = -2 ** 31
DSA_Q0 = 0
DSA_QI0 = ATT_HEADS * ATT_HEAD
DSA_K0 = DSA_QI0 + IDX_HEADS * IDX_DIM
DSA_V0 = DSA_K0 + ATT_HEAD
DSA_KIW0 = DSA_V0 + ATT_HEAD
DSA_COLS = DSA_KIW0 + LANES
DSA_KEY_CLASSES = 8


def _swap32(x):
    lane = lax.broadcasted_iota(jnp.int32, x.shape, 1)
    return jnp.where((lane % IDX_DIM) < IDX_DIM // 2, pltpu.roll(x, LANES - IDX_DIM // 2, 1),
                     pltpu.roll(x, IDX_DIM // 2, 1))


def _count_ge(key, cand):
    return jnp.sum(jnp.where(key >= cand, 1.0, 0.0), axis=1, keepdims=True)


def _dsa_body(n_sel, q_ref, qi_ref, kiwq_ref, k_ref, v_ref, kiw_ref, cqb_ref, sqb_ref, cib_ref, sib_ref,
              cq_ref, sq_ref, ci_ref, si_ref, o_ref, krot_ref, vbf_ref, kirot_ref):
    j = pl.program_id(1)
    T = k_ref.shape[0]
    QB = q_ref.shape[0]

    @pl.when(j == 0)
    def _():
        k = k_ref[...]
        krot_ref[...] = (k * cq_ref[...] + pltpu.roll(k, ATT_HEAD // 2, 1) * sq_ref[...]).astype(jnp.bfloat16)
        vbf_ref[...] = v_ref[...].astype(jnp.bfloat16)
        kiw = kiw_ref[...]
        kirot_ref[...] = (kiw * ci_ref[...] + _swap32(kiw) * si_ref[...]).astype(jnp.bfloat16)

    def attend(nk):
        col_bits = (nk - 1).bit_length()
        ci_b = cib_ref[...]
        si_b = sib_ref[...]
        ki = kirot_ref[:nk, :IDX_DIM]
        wi = kiwq_ref[:, IDX_DIM:IDX_DIM + IDX_HEADS] * (IDX_HEADS ** -0.5 * IDX_DIM ** -0.5)
        score = jnp.zeros((QB, nk), F32)
        for g in range(IDX_HEADS * IDX_DIM // LANES):
            qg = qi_ref[:, g * LANES:(g + 1) * LANES]
            qg = (qg * ci_b + _swap32(qg) * si_b).astype(jnp.bfloat16)
            for hh in range(LANES // IDX_DIM):
                h = g * (LANES // IDX_DIM) + hh
                s = lax.dot_general(qg[:, hh * IDX_DIM:(hh + 1) * IDX_DIM], ki, (((1,), (1,)), ((), ())),
                                    preferred_element_type=F32)
                score = score + wi[:, h:h + 1] * jnp.maximum(s, 0.0)
        col = lax.broadcasted_iota(jnp.int32, (QB, nk), 1)
        row = j * QB + lax.broadcasted_iota(jnp.int32, (QB, nk), 0)
        causal = col <= row
        score = jnp.where(causal, score, NEG)

        bits = pltpu.bitcast(score, jnp.int32)
        key = jnp.where(bits < 0, bits ^ jnp.int32(0x7FFFFFFF), bits)
        kf = float(n_sel)
        t0 = jnp.where(_count_ge(key, jnp.zeros((QB, 1), jnp.int32)) >= kf, jnp.int32(0), jnp.int32(INT_MIN))

        def bit_step(i, t):
            cand = t | (jnp.int32(1) << (30 - i))
            return jnp.where(_count_ge(key, cand) >= kf, cand, t)

        thr = lax.fori_loop(0, 31, bit_step, t0)
        gt = key > thr
        eq = key == thr
        need = kf - jnp.sum(jnp.where(gt, 1.0, 0.0), axis=1, keepdims=True)

        def tie_step(i, jj):
            cand = jj | (jnp.int32(1) << (col_bits - 1 - i))
            c = jnp.sum(jnp.where(eq & (col < cand), 1.0, 0.0), axis=1, keepdims=True)
            return jnp.where(c < need, cand, jj)

        n_eq = jnp.sum(jnp.where(eq, 1.0, 0.0), axis=1, keepdims=True)
        any_tie = jnp.max(jnp.where(n_eq > need, 1.0, 0.0)) > 0.0
        jstar = lax.cond(any_tie,
                         lambda: lax.fori_loop(0, col_bits, tie_step, jnp.zeros((QB, 1), jnp.int32)),
                         lambda: jnp.full((QB, 1), nk, jnp.int32))
        sel = (gt | (eq & (col <= jstar))) & causal
        bias = jnp.where(sel, 0.0, NEG)

        cq_b = cqb_ref[...]
        sq_b = sqb_ref[...]
        kr = krot_ref[:nk, :]
        vb = vbf_ref[:nk, :]
        for h in range(ATT_HEADS):
            qh = q_ref[:, h * ATT_HEAD:(h + 1) * ATT_HEAD]
            qh = (qh * cq_b + pltpu.roll(qh, ATT_HEAD // 2, 1) * sq_b).astype(jnp.bfloat16)
            logit = lax.dot_general(qh, kr, (((1,), (1,)), ((), ())), preferred_element_type=F32)
            logit = logit * (ATT_HEAD ** -0.5) + bias
            m = jnp.max(logit, axis=1, keepdims=True)
            p = jnp.exp(logit - m)
            l = jnp.sum(p, axis=1, keepdims=True)
            o = jnp.dot(p.astype(jnp.bfloat16), vb, preferred_element_type=F32)
            o_ref[:, h * ATT_HEAD:(h + 1) * ATT_HEAD] = o / l

    nq = T // QB
    n_cls = DSA_KEY_CLASSES if nq % DSA_KEY_CLASSES == 0 else 1
    per = nq // n_cls
    for c in range(n_cls):
        nk = max((c + 1) * per * QB, n_sel)
        pl.when((j >= c * per) & (j < (c + 1) * per))(functools.partial(attend, nk))


def dsa_attention(p, B, T):
    n_sel = min(TOPK_MAX, T // 4)
    nq = T // Q_BLOCK
    cos_a, sin_a = rope_tables(T, ATT_HEAD)
    cos_i, sin_i = rope_tables(T, IDX_DIM)
    cq = jnp.concatenate([cos_a, cos_a], -1)
    sq = jnp.concatenate([-sin_a, sin_a], -1)
    ci = jnp.concatenate([cos_i, cos_i, cos_i, cos_i], -1)
    si = jnp.concatenate([-sin_i, sin_i, -sin_i, sin_i], -1)

    def qrow(width, col0):
        return pl.BlockSpec((Q_BLOCK, width), lambda b, j: (b * nq + j, col0 // width))

    def krow(col0):
        return pl.BlockSpec((T, LANES), lambda b, j: (b, col0 // LANES))

    tab_q = pl.BlockSpec((Q_BLOCK, LANES), lambda b, j: (j, 0))
    tab_k = pl.BlockSpec((T, LANES), lambda b, j: (0, 0))
    return pl.pallas_call(
        functools.partial(_dsa_body, n_sel),
        grid=(B, nq),
        in_specs=[qrow(ATT_HEADS * ATT_HEAD, DSA_Q0), qrow(IDX_HEADS * IDX_DIM, DSA_QI0), qrow(LANES, DSA_KIW0),
                  krow(DSA_K0), krow(DSA_V0), krow(DSA_KIW0),
                  tab_q, tab_q, tab_q, tab_q, tab_k, tab_k, tab_k, tab_k],
        out_specs=pl.BlockSpec((Q_BLOCK, D_MODEL), lambda b, j: (b * nq + j, 0)),
        out_shape=jax.ShapeDtypeStruct((B * T, D_MODEL), F32),
        scratch_shapes=[pltpu.VMEM((T, ATT_HEAD), jnp.bfloat16), pltpu.VMEM((T, ATT_HEAD), jnp.bfloat16),
                        pltpu.VMEM((T, LANES), jnp.bfloat16)],
        compiler_params=pltpu.CompilerParams(dimension_semantics=("arbitrary", "arbitrary"),
                                             vmem_limit_bytes=VMEM_LIMIT),
        name="dsa_attention",
    )(p, p, p, p, p, p, cq, sq, ci, si, cq, sq, ci, si)


def dsa_mixer(x, w_in, w_out):
    B, T, D = x.shape
    wq, wk, wv, wqi, wki, wwi = _split(w_in, ODD_SPLITS)
    pad = jnp.zeros((D, DSA_COLS - DSA_KIW0 - IDX_DIM - IDX_HEADS), w_in.dtype)
    w_cat = jnp.concatenate([wq, wqi, wk, wv, wki, wwi, pad], axis=1)
    p = matmul(x.reshape(B * T, D), w_cat)
    o = dsa_attention(p, B, T)
    return matmul(o, w_out).reshape(B, T, D)


NPAIR = PEER_HEADS * PEER_TOPK
PEER_TB = 128
SLAB = 4
GS = 136
PAIRS_PER_ITER = 8
RT_TOKENS = 256
PEER_VMEM_LIMIT = 56 * 1024 * 1024
NEG_INF = float("-inf")


def pack_table(tab):
    E = tab.shape[0]
    b = lax.bitcast_convert_type(tab.astype(jnp.bfloat16), jnp.uint16).astype(jnp.uint32)
    b = b.reshape(E, SLAB, 2, LANES)
    w = b[:, :, 0, :] | (b[:, :, 1, :] << 16)
    return lax.bitcast_convert_type(w, jnp.int32).reshape(E * SLAB, LANES)


def _gather_rows(idx_ref, base, tab_ref, gt_ref):
    for p in range(NPAIR):
        i = pl.multiple_of(idx_ref[0, 0, base + p], SLAB)
        gt_ref[pl.ds(p, SLAB, stride=GS), :] = tab_ref[pl.ds(i, SLAB), :]


def _unpacked(gt_ref):
    parts = []
    for j in range(SLAB):
        w = gt_ref[pl.ds(j * GS, NPAIR), :]
        parts.append(pltpu.bitcast(w << 16, F32).astype(jnp.bfloat16))
        parts.append(pltpu.bitcast(w & jnp.int32(-65536), F32).astype(jnp.bfloat16))
    return jnp.concatenate(parts, axis=1)


def _token_loop(idx_ref, tab_ref, gta_ref, gtb_ref, compute):
    _gather_rows(idx_ref, 0, tab_ref, gta_ref)

    def pairs(i, c):
        for k in range(PAIRS_PER_ITER):
            t = 2 * (i * PAIRS_PER_ITER + k)
            _gather_rows(idx_ref, (t + 1) * NPAIR, tab_ref, gtb_ref)
            compute(t, _unpacked(gta_ref))
            _gather_rows(idx_ref, jnp.minimum(t + 2, PEER_TB - 1) * NPAIR, tab_ref, gta_ref)
            compute(t + 1, _unpacked(gtb_ref))
        return c

    lax.fori_loop(0, PEER_TB // (2 * PAIRS_PER_ITER), pairs, 0)


def _peer_u_body(idx_ref, x_ref, tab_ref, h_ref, gta_ref, gtb_ref):
    def compute(t, g):
        xt = x_ref[pl.ds(t, 1), :].astype(jnp.bfloat16)
        h_ref[pl.ds(t, 1), :] = lax.dot_general(xt, g, (((1,), (1,)), ((), ())), preferred_element_type=F32)

    _token_loop(idx_ref, tab_ref, gta_ref, gtb_ref, compute)


def _peer_v_body(idx_ref, h_ref, gate_ref, tab_ref, o_ref, coef_ref, gta_ref, gtb_ref):
    hv = h_ref[...]
    coef_ref[...] = gate_ref[...] * (0.5 * hv * (1.0 + lax.erf(hv * (2.0 ** -0.5))))

    def compute(t, g):
        ct = coef_ref[pl.ds(t, 1), :].astype(jnp.bfloat16)
        o_ref[pl.ds(t, 1), :] = jnp.dot(ct, g, preferred_element_type=F32)

    _token_loop(idx_ref, tab_ref, gta_ref, gtb_ref, compute)


def _peer_gather_call(body, name, idx4, acts, tab, out_w, extra_scratch):
    n = acts[0].shape[0]
    nb = n // PEER_TB
    gather_tile = pltpu.VMEM((SLAB * GS, LANES), jnp.int32)
    return pl.pallas_call(
        body,
        grid=(nb,),
        in_specs=[pl.BlockSpec((1, 1, PEER_TB * NPAIR), lambda i: (i, 0, 0), memory_space=pltpu.SMEM)]
        + [pl.BlockSpec((PEER_TB, a.shape[1]), lambda i: (i, 0)) for a in acts]
        + [pl.BlockSpec(memory_space=pltpu.VMEM)],
        out_specs=pl.BlockSpec((PEER_TB, out_w), lambda i: (i, 0)),
        out_shape=jax.ShapeDtypeStruct((n, out_w), F32),
        scratch_shapes=extra_scratch + [gather_tile, gather_tile],
        compiler_params=pltpu.CompilerParams(dimension_semantics=("arbitrary",),
                                             vmem_limit_bytes=PEER_VMEM_LIMIT),
        name=name,
    )(idx4.reshape(nb, 1, PEER_TB * NPAIR), *acts, tab)


def _top16(vals, ids):
    n, T = vals.shape
    pos = lax.broadcasted_iota(jnp.int32, (n, T), 0).astype(F32)
    slot = lax.broadcasted_iota(jnp.int32, (PEER_TOPK, T), 0)
    top_v = jnp.zeros((PEER_TOPK, T), F32)
    top_i = jnp.zeros((PEER_TOPK, T), F32)
    for r in range(PEER_TOPK):
        m = jnp.max(vals, axis=0, keepdims=True)
        am = jnp.min(jnp.where(vals == m, pos, float(n)), axis=0, keepdims=True)
        hit = pos == am
        sel = am if ids is None else jnp.max(jnp.where(hit, ids, -1.0), axis=0, keepdims=True)
        top_v = jnp.where(slot == r, m, top_v)
        top_i = jnp.where(slot == r, sel, top_i)
        vals = jnp.where(hit, NEG_INF, vals)
    return top_v, top_i


def _route_head(keys0, keys1, q0, q1):
    T = q0.shape[1]
    s0, i0 = _top16(jnp.dot(keys0, q0.astype(jnp.bfloat16), preferred_element_type=F32), None)
    s1, i1 = _top16(jnp.dot(keys1, q1.astype(jnp.bfloat16), preferred_element_type=F32), None)
    sub8 = lax.broadcasted_iota(jnp.int32, (8, T), 0)
    cs, cids = [s0[0:1, :] + s1], [i0[0:1, :] * PEER_NKEYS + i1]
    for a in range(1, 8):
        keep = sub8 < PEER_TOPK // (a + 1)
        cs.append(jnp.where(keep, s0[a:a + 1, :] + s1[0:8, :], NEG_INF))
        cids.append(i0[a:a + 1, :] * PEER_NKEYS + i1[0:8, :])
    cs.append(s0[8:16, :] + s1[0:1, :])
    cids.append(i0[8:16, :] * PEER_NKEYS + i1[0:1, :])
    best_s, best_i = _top16(jnp.concatenate(cs, axis=0), jnp.concatenate(cids, axis=0))
    e = jnp.exp(best_s - best_s[0:1, :])
    return (best_i * SLAB).astype(jnp.int32), e / jnp.sum(e, axis=0, keepdims=True)


def _route_body(x_ref, wqt_ref, keys_ref, idx_ref, gate_ref):
    qt = lax.dot_general(wqt_ref[...], x_ref[...].astype(jnp.bfloat16), (((1,), (1,)), ((), ())),
                         preferred_element_type=F32)
    half = PEER_DKEY // 2
    idx_rows, gate_rows = [], []
    for h in range(PEER_HEADS):
        g = 2 * h
        i_h, g_h = _route_head(keys_ref[g], keys_ref[g + 1], qt[g * half:(g + 1) * half, :],
                               qt[(g + 1) * half:(g + 2) * half, :])
        idx_rows.append(i_h)
        gate_rows.append(g_h)
    idx_ref[...] = jnp.concatenate(idx_rows, axis=0).T
    gate_ref[...] = jnp.concatenate(gate_rows, axis=0).T


def peer_route(x, w_q, sub_keys):
    n = x.shape[0]
    tb = min(RT_TOKENS, n)
    wqt = w_q.T.astype(jnp.bfloat16)
    keys = sub_keys.reshape(PEER_HEADS * 2, PEER_NKEYS, PEER_DKEY // 2).astype(jnp.bfloat16)
    return pl.pallas_call(
        _route_body,
        grid=(n // tb,),
        in_specs=[pl.BlockSpec((tb, D_MODEL), lambda i: (i, 0)),
                  pl.BlockSpec(wqt.shape, lambda i: (0, 0)),
                  pl.BlockSpec(keys.shape, lambda i: (0, 0, 0))],
        out_specs=[pl.BlockSpec((tb, NPAIR), lambda i: (i, 0)), pl.BlockSpec((tb, NPAIR), lambda i: (i, 0))],
        out_shape=[jax.ShapeDtypeStruct((n, NPAIR), jnp.int32), jax.ShapeDtypeStruct((n, NPAIR), F32)],
        compiler_params=pltpu.CompilerParams(dimension_semantics=("arbitrary",), vmem_limit_bytes=PEER_VMEM_LIMIT),
        name="peer_route",
    )(x, wqt, keys)


def peer_ffn(x, w_q, sub_keys, exp_u, exp_v):
    B, T, D = x.shape
    xf = x.reshape(B * T, D)
    idx4, gate = peer_route(xf, w_q, sub_keys)
    h = _peer_gather_call(_peer_u_body, "peer_u", idx4, [xf], pack_table(exp_u), NPAIR, [])
    out = _peer_gather_call(_peer_v_body, "peer_v", idx4, [h, gate], pack_table(exp_v), D,
                            [pltpu.VMEM((PEER_TB, NPAIR), F32)])
    return out.reshape(B, T, D)


def kernel(x, even_w_in, gla_a_w2, gla_a_b, gla_norm_g, rwkv_mu, rwkv_w0, rwkv_w2, rwkv_a0, rwkv_a2,
           rwkv_g2, rwkv_k_k, rwkv_k_a, rwkv_r_k, rwkv_lnx_g, rwkv_lnx_b, even_w_out, odd_w_in,
           odd_w_out, mix_ln_g, mix_ln_b, peer_w_q, peer_sub_keys, peer_u, peer_v, ffn_ln_g, ffn_ln_b):
    for layer in range(DEPTH):
        i = layer // 2
        if layer % 2 == 0:
            mix = even_mixer(x, even_w_in[i], gla_a_w2[i], gla_a_b[i], gla_norm_g[i], rwkv_mu[i], rwkv_w0[i],
                             rwkv_w2[i], rwkv_a0[i], rwkv_a2[i], rwkv_g2[i], rwkv_k_k[i], rwkv_k_a[i],
                             rwkv_r_k[i], rwkv_lnx_g[i], rwkv_lnx_b[i], even_w_out[i])
        else:
            mix = dsa_mixer(x, odd_w_in[i], odd_w_out[i])
        x = add_layer_norm(x, mix, mix_ln_g[layer], mix_ln_b[layer])
        ffn = peer_ffn(x, peer_w_q[layer], peer_sub_keys[layer], peer_u[layer], peer_v[layer])
        x = add_layer_norm(x, ffn, ffn_ln_g[layer], ffn_ln_b[layer])
    return x
```

```python
import functools

import jax
import jax.numpy as jnp
import numpy as np
from jax import lax
from jax.experimental import pallas as pl
from jax.experimental.pallas import tpu as pltpu

D_MODEL = 1024
BATCH = 32
SEQ = 2048
DEPTH = 4

GLA_HEADS = 4
GLA_VW = D_MODEL // 2
GLA_DV = GLA_VW // GLA_HEADS
GLA_KW = GLA_VW // 2
GLA_DK = GLA_KW // GLA_HEADS
GLA_LORA = 16
GLA_TAU = 16.0
GLA_CHUNK = 64
RWKV_W = D_MODEL - GLA_VW
RWKV_HEAD = 64
RWKV_HEADS = RWKV_W // RWKV_HEAD
RWKV_W_LORA = 32
RWKV_A_LORA = 32
RWKV_G_LORA = 96
RWKV_LNX_EPS = 64e-5
ATT_HEADS = 8
ATT_HEAD = D_MODEL // ATT_HEADS
IDX_HEADS = 8
IDX_DIM = 64
TOPK_MAX = 256
Q_BLOCK = 128
ROPE_THETA = 10000.0
NEG = -1e30
PEER_HEADS = 8
PEER_DKEY = 128
PEER_NKEYS = 128
PEER_N_EXPERTS = PEER_NKEYS * PEER_NKEYS
PEER_TOPK = 16
PEER_TOKEN_BLOCK = 128
DN_ALPHA = (2 * DEPTH) ** 0.25
DN_BETA = (8 * DEPTH) ** -0.25
LN_EPS = 1e-5
N_EVEN = (DEPTH + 1) // 2
N_ODD = DEPTH // 2

GLA_SPLITS = (GLA_KW, GLA_KW, GLA_VW, GLA_VW, GLA_LORA)
RWKV_SPLITS = (RWKV_W, RWKV_W, RWKV_W, RWKV_W_LORA, RWKV_A_LORA, RWKV_G_LORA)
GLA_IN = sum(GLA_SPLITS)
RWKV_IN = sum(RWKV_SPLITS)
EVEN_IN = GLA_IN + RWKV_IN
ODD_SPLITS = (ATT_HEADS * ATT_HEAD, ATT_HEAD, ATT_HEAD, IDX_HEADS * IDX_DIM, IDX_DIM, IDX_HEADS)
ODD_IN = sum(ODD_SPLITS)

F32 = jnp.float32


def _split(t, sizes):
    cuts = [int(c) for c in np.cumsum(sizes)[:-1]]
    return jnp.split(t, cuts, axis=-1)


LN_ROWS = 512


def _add_ln_body(x_ref, y_ref, g_ref, b_ref, o_ref):
    z = DN_ALPHA * x_ref[...] + y_ref[...]
    mu = jnp.mean(z, axis=-1, keepdims=True)
    zc = z - mu
    var = jnp.mean(zc * zc, axis=-1, keepdims=True)
    o_ref[...] = zc * lax.rsqrt(var + LN_EPS) * g_ref[...] + b_ref[...]


def add_layer_norm(x, y, g, b):
    B, T, D = x.shape
    n = B * T
    row = pl.BlockSpec((LN_ROWS, D), lambda i: (i, 0))
    vec = pl.BlockSpec((1, D), lambda i: (0, 0))
    out = pl.pallas_call(
        _add_ln_body,
        grid=(n // LN_ROWS,),
        in_specs=[row, row, vec, vec],
        out_specs=row,
        out_shape=jax.ShapeDtypeStruct((n, D), F32),
        name="add_layer_norm",
    )(x.reshape(n, D), y.reshape(n, D), g.reshape(1, D), b.reshape(1, D))
    return out.reshape(B, T, D)


def rope_tables(T, dim):
    inv = ROPE_THETA ** (-jnp.arange(0, dim, 2, dtype=F32) / dim)
    ang = jnp.arange(T, dtype=F32)[:, None] * inv[None, :]
    return jnp.cos(ang), jnp.sin(ang)


def apply_rope(x, cos, sin):
    x1, x2 = jnp.split(x, 2, axis=-1)
    return jnp.concatenate([x1 * cos - x2 * sin, x2 * cos + x1 * sin], -1).astype(x.dtype)


GLA_ROWS = 256


def _gla_body(q_ref, k_ref, v_ref, g_ref, al_ref, w2_ref, ab_ref, ng_ref, o_ref, st_ref):
    C = GLA_CHUNK
    bf = jnp.bfloat16

    @pl.when(pl.program_id(1) == 0)
    def _():
        st_ref[...] = jnp.zeros_like(st_ref)

    z = jnp.dot(al_ref[:, :GLA_LORA].astype(bf), w2_ref[...], preferred_element_type=F32) + ab_ref[...]
    la = -(jnp.maximum(-z, 0.0) + jnp.log1p(jnp.exp(-jnp.abs(z)))) / GLA_TAU
    row = lax.broadcasted_iota(jnp.int32, (C, GLA_KW), 0)
    tril = lax.broadcasted_iota(jnp.int32, (C, C), 0) >= lax.broadcasted_iota(jnp.int32, (C, C), 1)
    for n in range(q_ref.shape[0] // C):
        sl = slice(n * C, (n + 1) * C)
        b = la[sl]
        sh = 1
        while sh < C:
            b = b + jnp.where(row >= sh, pltpu.roll(b, sh, 0), 0.0)
            sh *= 2
        b_last = b[C - 1:C, :]
        k_c = k_ref[sl, :]
        q_in = q_ref[sl, :] * (GLA_DK ** -0.5) * jnp.exp(b)
        k_in = k_c * jnp.exp(-b)
        k_out = k_c * jnp.exp(b_last - b)
        dec = jnp.exp(b_last)
        for h in range(GLA_HEADS):
            ks = slice(h * GLA_DK, (h + 1) * GLA_DK)
            vs = slice(h * GLA_DV, (h + 1) * GLA_DV)
            qh = q_in[:, ks].astype(bf)
            vh = v_ref[sl, vs].astype(bf)
            a = lax.dot_general(qh, k_in[:, ks].astype(bf), (((1,), (1,)), ((), ())), preferred_element_type=F32)
            a = jnp.where(tril, a, 0.0)
            st = st_ref[h]
            o = jnp.dot(a.astype(bf), vh, preferred_element_type=F32)
            o = o + lax.dot_general(qh, st.astype(bf), (((1,), (1,)), ((), ())), preferred_element_type=F32)
            kv_t = lax.dot_general(vh, k_out[:, ks].astype(bf), (((0,), (0,)), ((), ())),
                                   preferred_element_type=F32)
            st_ref[h] = st * dec[:, ks] + kv_t
            o = o * lax.rsqrt(jnp.mean(o * o, axis=-1, keepdims=True) + LN_EPS) * ng_ref[...]
            gate = g_ref[sl, vs]
            o_ref[sl, vs] = o * (gate / (1.0 + jnp.exp(-gate)))


def gla_attention(p, a_w2, a_b, norm_g, B, T):
    R = min(GLA_ROWS, T)
    nr = T // R

    def cols(width, col0):
        return pl.BlockSpec((R, width), lambda b, i: (b * nr + i, col0 // width))

    def whole(shape):
        return pl.BlockSpec(shape, lambda b, i: (0,) * len(shape))

    return pl.pallas_call(
        _gla_body,
        grid=(B, nr),
        in_specs=[cols(GLA_KW, 0), cols(GLA_KW, GLA_KW), cols(GLA_VW, 2 * GLA_KW), cols(GLA_VW, 2 * GLA_KW + GLA_VW),
                  cols(LANES, 2 * GLA_KW + 2 * GLA_VW),
                  whole((GLA_LORA, GLA_KW)), whole((1, GLA_KW)), whole((1, GLA_DV))],
        out_specs=pl.BlockSpec((R, GLA_VW), lambda b, i: (b * nr + i, 0)),
        out_shape=jax.ShapeDtypeStruct((B * T, GLA_VW), F32),
        scratch_shapes=[pltpu.VMEM((GLA_HEADS, GLA_DV, GLA_DK), F32)],
        compiler_params=pltpu.CompilerParams(dimension_semantics=("arbitrary", "arbitrary"),
                                             vmem_limit_bytes=VMEM_LIMIT),
        name="gla_attention",
    )(p, p, p, p, p, a_w2.astype(jnp.bfloat16), a_b.reshape(1, GLA_KW), norm_g.reshape(1, GLA_DV))


SCAN_TC = 16


def _rwkv_scan_body(r_ref, w_ref, k_ref, v_ref, a_ref, b_ref, y_ref, s_ref):
    N = RWKV_HEAD

    @pl.when(pl.program_id(1) == 0)
    def _():
        s_ref[...] = jnp.zeros_like(s_ref)

    def step(tt, c):
        sa = jnp.zeros((N, LANES), F32)
        for j in range(N):
            sa = sa + s_ref[j] * a_ref[tt, j:j + 1, :]
        v_t = v_ref[tt]
        y = jnp.zeros((N, LANES), F32)
        for j in range(N):
            sj = s_ref[j] * w_ref[tt, j:j + 1, :] + sa * b_ref[tt, j:j + 1, :] + v_t * k_ref[tt, j:j + 1, :]
            s_ref[j] = sj
            y = y + sj * r_ref[tt, j:j + 1, :]
        y_ref[tt] = y
        return c

    lax.fori_loop(0, r_ref.shape[0], step, 0)


def rwkv7_scan(r, w, k, v, a, b):
    B, T, H, N = r.shape
    C = B * H
    tc = min(SCAN_TC, T)

    def to_scan(t):
        return t.astype(F32).transpose(1, 3, 0, 2).reshape(T, N, C)

    spec = pl.BlockSpec((tc, N, LANES), lambda g, i: (i, 0, g))
    y = pl.pallas_call(
        _rwkv_scan_body,
        grid=(C // LANES, T // tc),
        in_specs=[spec] * 6,
        out_specs=spec,
        out_shape=jax.ShapeDtypeStruct((T, N, C), F32),
        scratch_shapes=[pltpu.VMEM((N, N, LANES), F32)],
        compiler_params=pltpu.CompilerParams(dimension_semantics=("arbitrary", "arbitrary"),
                                             vmem_limit_bytes=VMEM_LIMIT),
        name="rwkv7_scan",
    )(*(to_scan(t) for t in (r, w, k, v, a, b)))
    return y.reshape(T, N, B, H).transpose(2, 0, 3, 1)


def even_mixer(x, w_in, gla_a_w2, gla_a_b, gla_norm_g, rwkv_mu, rwkv_w0, rwkv_w2, rwkv_a0, rwkv_a2,
               rwkv_g2, rwkv_k_k, rwkv_k_a, rwkv_r_k, rwkv_lnx_g, rwkv_lnx_b, w_out):
    B, T, _ = x.shape

    def heads(t, h):
        return t.reshape(B, T, h, -1)

    p = matmul(x.reshape(B * T, -1), w_in).reshape(B, T, -1)
    gla_p, rw_p = p[..., :GLA_IN], p[..., GLA_IN:]
    gla_out = gla_attention(p.reshape(B * T, -1), gla_a_w2, gla_a_b, gla_norm_g, B, T).reshape(B, T, GLA_VW)
    prev = jnp.pad(rw_p, ((0, 0), (1, 0), (0, 0)))[:, :-1]
    rw_p = rw_p + (prev - rw_p) * rwkv_mu
    r, k, v, wl, al, gl = _split(rw_p, RWKV_SPLITS)
    w_log = -jax.nn.softplus(-(rwkv_w0 + jnp.tanh(wl) @ rwkv_w2).astype(F32)) - 0.5
    decay = jnp.exp(-jnp.exp(w_log))
    a = jax.nn.sigmoid((rwkv_a0 + al @ rwkv_a2).astype(F32))
    g = jax.nn.sigmoid(gl) @ rwkv_g2
    kk = heads((k * rwkv_k_k).astype(F32), RWKV_HEADS)
    kk = kk / jnp.maximum(jnp.sqrt(jnp.sum(jnp.square(kk), -1, keepdims=True)), 1e-12)
    k = k * (1.0 + (a - 1.0) * rwkv_k_a)
    a_h = heads(a, RWKV_HEADS)
    r_h, k_h, v_h = heads(r, RWKV_HEADS), heads(k, RWKV_HEADS), heads(v, RWKV_HEADS)
    y = rwkv7_scan(r_h, heads(decay, RWKV_HEADS), k_h, v_h, -kk, kk * a_h)
    mu = y.mean(-1, keepdims=True)
    var = jnp.square(y - mu).mean(-1, keepdims=True)
    y = (y - mu) * lax.rsqrt(var + RWKV_LNX_EPS) * rwkv_lnx_g.reshape(RWKV_HEADS, RWKV_HEAD) \
        + rwkv_lnx_b.reshape(RWKV_HEADS, RWKV_HEAD)
    bonus = jnp.sum((r_h * k_h * rwkv_r_k).astype(F32), -1, keepdims=True) * v_h.astype(F32)
    rwkv_out = ((y + bonus).reshape(B, T, RWKV_W) * g.astype(F32)).astype(x.dtype)
    return matmul(jnp.concatenate([gla_out, rwkv_out], -1).reshape(B * T, -1), w_out).reshape(B, T, -1)


MM_ROWS = 512
VMEM_LIMIT = 48 * 1024 * 1024


def _mm_body(x_ref, w_ref, o_ref):
    o_ref[...] = jnp.dot(x_ref[...].astype(jnp.bfloat16), w_ref[...], preferred_element_type=F32)


def matmul(x, w):
    M, K = x.shape
    N = w.shape[1]
    tm = min(MM_ROWS, M)
    return pl.pallas_call(
        _mm_body,
        grid=(M // tm,),
        in_specs=[pl.BlockSpec((tm, K), lambda i: (i, 0)), pl.BlockSpec((K, N), lambda i: (0, 0))],
        out_specs=pl.BlockSpec((tm, N), lambda i: (i, 0)),
        out_shape=jax.ShapeDtypeStruct((M, N), F32),
        compiler_params=pltpu.CompilerParams(vmem_limit_bytes=VMEM_LIMIT),
        name="matmul",
    )(x, w.astype(jnp.bfloat16))


LANES = 128
INT_MIN = -2 ** 31
DSA_Q0 = 0
DSA_QI0 = ATT_HEADS * ATT_HEAD
DSA_K0 = DSA_QI0 + IDX_HEADS * IDX_DIM
DSA_V0 = DSA_K0 + ATT_HEAD
DSA_KIW0 = DSA_V0 + ATT_HEAD
DSA_COLS = DSA_KIW0 + LANES
DSA_KEY_CLASSES = 8


def _swap32(x):
    lane = lax.broadcasted_iota(jnp.int32, x.shape, 1)
    return jnp.where((lane % IDX_DIM) < IDX_DIM // 2, pltpu.roll(x, LANES - IDX_DIM // 2, 1),
                     pltpu.roll(x, IDX_DIM // 2, 1))


def _count_ge(key, cand):
    return jnp.sum(jnp.where(key >= cand, 1.0, 0.0), axis=1, keepdims=True)


def _dsa_body(n_sel, q_ref, qi_ref, kiwq_ref, k_ref, v_ref, kiw_ref, cqb_ref, sqb_ref, cib_ref, sib_ref,
              cq_ref, sq_ref, ci_ref, si_ref, o_ref, krot_ref, vbf_ref, kirot_ref):
    j = pl.program_id(1)
    T = k_ref.shape[0]
    QB = q_ref.shape[0]

    @pl.when(j == 0)
    def _():
        k = k_ref[...]
        krot_ref[...] = (k * cq_ref[...] + pltpu.roll(k, ATT_HEAD // 2, 1) * sq_ref[...]).astype(jnp.bfloat16)
        vbf_ref[...] = v_ref[...].astype(jnp.bfloat16)
        kiw = kiw_ref[...]
        kirot_ref[...] = (kiw * ci_ref[...] + _swap32(kiw) * si_ref[...]).astype(jnp.bfloat16)

    def attend(nk):
        col_bits = (nk - 1).bit_length()
        ci_b = cib_ref[...]
        si_b = sib_ref[...]
        ki = kirot_ref[:nk, :IDX_DIM]
        wi = kiwq_ref[:, IDX_DIM:IDX_DIM + IDX_HEADS] * (IDX_HEADS ** -0.5 * IDX_DIM ** -0.5)
        score = jnp.zeros((QB, nk), F32)
        for g in range(IDX_HEADS * IDX_DIM // LANES):
            qg = qi_ref[:, g * LANES:(g + 1) * LANES]
            qg = (qg * ci_b + _swap32(qg) * si_b).astype(jnp.bfloat16)
            for hh in range(LANES // IDX_DIM):
                h = g * (LANES // IDX_DIM) + hh
                s = lax.dot_general(qg[:, hh * IDX_DIM:(hh + 1) * IDX_DIM], ki, (((1,), (1,)), ((), ())),
                                    preferred_element_type=F32)
                score = score + wi[:, h:h + 1] * jnp.maximum(s, 0.0)
        col = lax.broadcasted_iota(jnp.int32, (QB, nk), 1)
        row = j * QB + lax.broadcasted_iota(jnp.int32, (QB, nk), 0)
        causal = col <= row
        score = jnp.where(causal, score, NEG)

        bits = pltpu.bitcast(score, jnp.int32)
        key = jnp.where(bits < 0, bits ^ jnp.int32(0x7FFFFFFF), bits)
        kf = float(n_sel)
        t0 = jnp.where(_count_ge(key, jnp.zeros((QB, 1), jnp.int32)) >= kf, jnp.int32(0), jnp.int32(INT_MIN))

        def bit_step(i, t):
            cand = t | (jnp.int32(1) << (30 - i))
            return jnp.where(_count_ge(key, cand) >= kf, cand, t)

        thr = lax.fori_loop(0, 31, bit_step, t0)
        gt = key > thr
        eq = key == thr
        need = kf - jnp.sum(jnp.where(gt, 1.0, 0.0), axis=1, keepdims=True)

        def tie_step(i, jj):
            cand = jj | (jnp.int32(1) << (col_bits - 1 - i))
            c = jnp.sum(jnp.where(eq & (col < cand), 1.0, 0.0), axis=1, keepdims=True)
            return jnp.where(c < need, cand, jj)

        n_eq = jnp.sum(jnp.where(eq, 1.0, 0.0), axis=1, keepdims=True)
        any_tie = jnp.max(jnp.where(n_eq > need, 1.0, 0.0)) > 0.0
        jstar = lax.cond(any_tie,
                         lambda: lax.fori_loop(0, col_bits, tie_step, jnp.zeros((QB, 1), jnp.int32)),
                         lambda: jnp.full((QB, 1), nk, jnp.int32))
        sel = (gt | (eq & (col <= jstar))) & causal
        bias = jnp.where(sel, 0.0, NEG)

        cq_b = cqb_ref[...]
        sq_b = sqb_ref[...]
        kr = krot_ref[:nk, :]
        vb = vbf_ref[:nk, :]
        for h in range(ATT_HEADS):
            qh = q_ref[:, h * ATT_HEAD:(h + 1) * ATT_HEAD]
            qh = (qh * cq_b + pltpu.roll(qh, ATT_HEAD // 2, 1) * sq_b).astype(jnp.bfloat16)
            logit = lax.dot_general(qh, kr, (((1,), (1,)), ((), ())), preferred_element_type=F32)
            logit = logit * (ATT_HEAD ** -0.5) + bias
            m = jnp.max(logit, axis=1, keepdims=True)
            p = jnp.exp(logit - m)
            l = jnp.sum(p, axis=1, keepdims=True)
            o = jnp.dot(p.astype(jnp.bfloat16), vb, preferred_element_type=F32)
            o_ref[:, h * ATT_HEAD:(h + 1) * ATT_HEAD] = o / l

    nq = T // QB
    n_cls = DSA_KEY_CLASSES if nq % DSA_KEY_CLASSES == 0 else 1
    per = nq // n_cls
    for c in range(n_cls):
        nk = max((c + 1) * per * QB, n_sel)
        pl.when((j >= c * per) & (j < (c + 1) * per))(functools.partial(attend, nk))


def dsa_attention(p, B, T):
    n_sel = min(TOPK_MAX, T // 4)
    nq = T // Q_BLOCK
    cos_a, sin_a = rope_tables(T, ATT_HEAD)
    cos_i, sin_i = rope_tables(T, IDX_DIM)
    cq = jnp.concatenate([cos_a, cos_a], -1)
    sq = jnp.concatenate([-sin_a, sin_a], -1)
    ci = jnp.concatenate([cos_i, cos_i, cos_i, cos_i], -1)
    si = jnp.concatenate([-sin_i, sin_i, -sin_i, sin_i], -1)

    def qrow(width, col0):
        return pl.BlockSpec((Q_BLOCK, width), lambda b, j: (b * nq + j, col0 // width))

    def krow(col0):
        return pl.BlockSpec((T, LANES), lambda b, j: (b, col0 // LANES))

    tab_q = pl.BlockSpec((Q_BLOCK, LANES), lambda b, j: (j, 0))
    tab_k = pl.BlockSpec((T, LANES), lambda b, j: (0, 0))
    return pl.pallas_call(
        functools.partial(_dsa_body, n_sel),
        grid=(B, nq),
        in_specs=[qrow(ATT_HEADS * ATT_HEAD, DSA_Q0), qrow(IDX_HEADS * IDX_DIM, DSA_QI0), qrow(LANES, DSA_KIW0),
                  krow(DSA_K0), krow(DSA_V0), krow(DSA_KIW0),
                  tab_q, tab_q, tab_q, tab_q, tab_k, tab_k, tab_k, tab_k],
        out_specs=pl.BlockSpec((Q_BLOCK, D_MODEL), lambda b, j: (b * nq + j, 0)),
        out_shape=jax.ShapeDtypeStruct((B * T, D_MODEL), F32),
        scratch_shapes=[pltpu.VMEM((T, ATT_HEAD), jnp.bfloat16), pltpu.VMEM((T, ATT_HEAD), jnp.bfloat16),
                        pltpu.VMEM((T, LANES), jnp.bfloat16)],
        compiler_params=pltpu.CompilerParams(dimension_semantics=("arbitrary", "arbitrary"),
                                             vmem_limit_bytes=VMEM_LIMIT),
        name="dsa_attention",
    )(p, p, p, p, p, p, cq, sq, ci, si, cq, sq, ci, si)


def dsa_mixer(x, w_in, w_out):
    B, T, D = x.shape
    wq, wk, wv, wqi, wki, wwi = _split(w_in, ODD_SPLITS)
    pad = jnp.zeros((D, DSA_COLS - DSA_KIW0 - IDX_DIM - IDX_HEADS), w_in.dtype)
    w_cat = jnp.concatenate([wq, wqi, wk, wv, wki, wwi, pad], axis=1)
    p = matmul(x.reshape(B * T, D), w_cat)
    o = dsa_attention(p, B, T)
    return matmul(o, w_out).reshape(B, T, D)


NPAIR = PEER_HEADS * PEER_TOPK
PEER_TB = 128
SLAB = 4
GS = 136
PAIRS_PER_ITER = 8
GATHER_CHUNKS = 4
PEER_VMEM_LIMIT = 56 * 1024 * 1024
NEG_INF = float("-inf")


def pack_table(tab):
    E = tab.shape[0]
    b = lax.bitcast_convert_type(tab.astype(jnp.bfloat16), jnp.uint16).astype(jnp.uint32)
    b = b.reshape(E, SLAB, 2, LANES)
    w = b[:, :, 0, :] | (b[:, :, 1, :] << 16)
    return lax.bitcast_convert_type(w, jnp.int32).reshape(E * SLAB, LANES)


def _gather_rows(idx_ref, base, tab_ref, gt_ref):
    for p in range(NPAIR):
        i = pl.multiple_of(idx_ref[0, 0, base + p], SLAB)
        gt_ref[pl.ds(p, SLAB, stride=GS), :] = tab_ref[pl.ds(i, SLAB), :]


def _unpacked(gt_ref):
    parts = []
    for j in range(SLAB):
        w = gt_ref[pl.ds(j * GS, NPAIR), :]
        parts.append(pltpu.bitcast(w << 16, F32).astype(jnp.bfloat16))
        parts.append(pltpu.bitcast(w & jnp.int32(-65536), F32).astype(jnp.bfloat16))
    return jnp.concatenate(parts, axis=1)


def _token_loop(idx_ref, tab_ref, gta_ref, gtb_ref, compute):
    _gather_rows(idx_ref, 0, tab_ref, gta_ref)

    def pairs(i, c):
        for k in range(PAIRS_PER_ITER):
            t = 2 * (i * PAIRS_PER_ITER + k)
            _gather_rows(idx_ref, (t + 1) * NPAIR, tab_ref, gtb_ref)
            compute(t, _unpacked(gta_ref))
            _gather_rows(idx_ref, jnp.minimum(t + 2, PEER_TB - 1) * NPAIR, tab_ref, gta_ref)
            compute(t + 1, _unpacked(gtb_ref))
        return c

    lax.fori_loop(0, PEER_TB // (2 * PAIRS_PER_ITER), pairs, 0)


def _peer_v_body(idx_ref, h_ref, gate_ref, tab_ref, o_ref, coef_ref, gta_ref, gtb_ref):
    hv = h_ref[...]
    coef_ref[...] = gate_ref[...] * (0.5 * hv * (1.0 + lax.erf(hv * (2.0 ** -0.5))))

    def compute(t, g):
        ct = coef_ref[pl.ds(t, 1), :].astype(jnp.bfloat16)
        o_ref[pl.ds(t, 1), :] = jnp.dot(ct, g, preferred_element_type=F32)

    _token_loop(idx_ref, tab_ref, gta_ref, gtb_ref, compute)


def _peer_gather_call(body, name, idx4, acts, tab, out_w, extra_scratch):
    n = acts[0].shape[0]
    nb = n // PEER_TB
    gather_tile = pltpu.VMEM((SLAB * GS, LANES), jnp.int32)
    return pl.pallas_call(
        body,
        grid=(nb,),
        in_specs=[pl.BlockSpec((1, 1, PEER_TB * NPAIR), lambda i: (i, 0, 0), memory_space=pltpu.SMEM)]
        + [pl.BlockSpec((PEER_TB, a.shape[1]), lambda i: (i, 0)) for a in acts]
        + [pl.BlockSpec(memory_space=pltpu.VMEM)],
        out_specs=pl.BlockSpec((PEER_TB, out_w), lambda i: (i, 0)),
        out_shape=jax.ShapeDtypeStruct((n, out_w), F32),
        scratch_shapes=extra_scratch + [gather_tile, gather_tile],
        compiler_params=pltpu.CompilerParams(dimension_semantics=("arbitrary",),
                                             vmem_limit_bytes=PEER_VMEM_LIMIT),
        name=name,
    )(idx4.reshape(nb, 1, PEER_TB * NPAIR), *acts, tab)


class _Stepper:
    def __init__(self, gen):
        self.gen, self.done, self.value = gen, False, None

    def step(self, n=1):
        for _ in range(n):
            if self.done:
                return
            try:
                next(self.gen)
            except StopIteration as e:
                self.done, self.value = True, e.value

    def finish(self):
        while not self.done:
            self.step()
        return self.value


def _top16_steps(vals, ids):
    n, T = vals.shape
    pos = lax.broadcasted_iota(jnp.int32, (n, T), 0).astype(F32)
    slot = lax.broadcasted_iota(jnp.int32, (PEER_TOPK, T), 0)
    top_v = jnp.zeros((PEER_TOPK, T), F32)
    top_i = jnp.zeros((PEER_TOPK, T), F32)
    for r in range(PEER_TOPK):
        m = jnp.max(vals, axis=0, keepdims=True)
        am = jnp.min(jnp.where(vals == m, pos, float(n)), axis=0, keepdims=True)
        hit = pos == am
        sel = am if ids is None else jnp.max(jnp.where(hit, ids, -1.0), axis=0, keepdims=True)
        top_v = jnp.where(slot == r, m, top_v)
        top_i = jnp.where(slot == r, sel, top_i)
        vals = jnp.where(hit, NEG_INF, vals)
        yield
    return top_v, top_i


def _route_head_steps(keys0, keys1, q0, q1):
    T = q0.shape[1]
    s0, i0 = yield from _top16_steps(jnp.dot(keys0, q0.astype(jnp.bfloat16), preferred_element_type=F32), None)
    s1, i1 = yield from _top16_steps(jnp.dot(keys1, q1.astype(jnp.bfloat16), preferred_element_type=F32), None)
    sub8 = lax.broadcasted_iota(jnp.int32, (8, T), 0)
    cs, cids = [s0[0:1, :] + s1], [i0[0:1, :] * PEER_NKEYS + i1]
    for a in range(1, 8):
        keep = sub8 < PEER_TOPK // (a + 1)
        cs.append(jnp.where(keep, s0[a:a + 1, :] + s1[0:8, :], NEG_INF))
        cids.append(i0[a:a + 1, :] * PEER_NKEYS + i1[0:8, :])
    cs.append(s0[8:16, :] + s1[0:1, :])
    cids.append(i0[8:16, :] * PEER_NKEYS + i1[0:1, :])
    best_s, best_i = yield from _top16_steps(jnp.concatenate(cs, axis=0), jnp.concatenate(cids, axis=0))
    e = jnp.exp(best_s - best_s[0:1, :])
    return (best_i * SLAB).astype(jnp.int32), e / jnp.sum(e, axis=0, keepdims=True)


def _route_u_body(xr_ref, xg_ref, wqt_ref, keys_ref, tab_ref, idx_ref, gate_ref, h_ref,
                  qt_ref, idxt_ref, gatet_ref, stage_ref, idx_smem, sem, gta_ref, gtb_ref):
    s = pl.program_id(0)
    half = PEER_DKEY // 2

    def to_smem():
        cp = pltpu.make_async_copy(stage_ref, idx_smem, sem)
        cp.start()
        cp.wait()

    @pl.when(s == 0)
    def _():
        stage_ref[...] = jnp.zeros_like(stage_ref)
        to_smem()

    qt_ref[...] = lax.dot_general(wqt_ref[...], xr_ref[...].astype(jnp.bfloat16), (((1,), (1,)), ((), ())),
                                  preferred_element_type=F32)

    def gather(t, gt_ref, lo=0, hi=NPAIR):
        for p in range(lo, hi):
            i = pl.multiple_of(idx_smem[t, p], SLAB)
            gt_ref[pl.ds(p, SLAB, stride=GS), :] = tab_ref[pl.ds(i, SLAB), :]

    def compute(t, g):
        xt = xg_ref[pl.ds(t, 1), :].astype(jnp.bfloat16)
        h_ref[pl.ds(t, 1), :] = lax.dot_general(xt, g, (((1,), (1,)), ((), ())), preferred_element_type=F32)

    gather(0, gta_ref)

    def iteration(it, c):
        q0 = qt_ref[pl.ds(pl.multiple_of(2 * it * half, half), half), :]
        q1 = qt_ref[pl.ds(pl.multiple_of((2 * it + 1) * half, half), half), :]
        route = _Stepper(_route_head_steps(keys_ref[2 * it], keys_ref[2 * it + 1], q0, q1))
        chunk = NPAIR // GATHER_CHUNKS
        for k in range(PAIRS_PER_ITER):
            t = 2 * (it * PAIRS_PER_ITER + k)
            for c4 in range(GATHER_CHUNKS):
                gather(t + 1, gtb_ref, c4 * chunk, (c4 + 1) * chunk)
                route.step()
            compute(t, _unpacked(gta_ref))
            nxt = jnp.minimum(t + 2, PEER_TB - 1)
            for c4 in range(GATHER_CHUNKS):
                gather(nxt, gta_ref, c4 * chunk, (c4 + 1) * chunk)
                route.step()
            compute(t + 1, _unpacked(gtb_ref))
        i_h, g_h = route.finish()
        rows = pl.ds(pl.multiple_of(it * PEER_TOPK, PEER_TOPK), PEER_TOPK)
        idxt_ref[rows, :] = i_h
        gatet_ref[rows, :] = g_h
        return c

    lax.fori_loop(0, PEER_HEADS, iteration, 0)
    idx_tok = idxt_ref[...].T
    idx_ref[...] = idx_tok
    gate_ref[...] = gatet_ref[...].T
    stage_ref[...] = idx_tok
    to_smem()


def peer_route_u(x, w_q, sub_keys, tab_u):
    n = x.shape[0]
    nb = n // PEER_TB
    assert PEER_TB == 2 * PAIRS_PER_ITER * PEER_HEADS and NPAIR == PEER_TB
    wqt = w_q.T.astype(jnp.bfloat16)
    keys = sub_keys.reshape(PEER_HEADS * 2, PEER_NKEYS, PEER_DKEY // 2).astype(jnp.bfloat16)
    cur = lambda s: (jnp.minimum(s, nb - 1), 0)
    prev = lambda s: (jnp.maximum(s - 1, 0), 0)
    gather_tile = pltpu.VMEM((SLAB * GS, LANES), jnp.int32)
    return pl.pallas_call(
        _route_u_body,
        grid=(nb + 1,),
        in_specs=[pl.BlockSpec((PEER_TB, D_MODEL), cur), pl.BlockSpec((PEER_TB, D_MODEL), prev),
                  pl.BlockSpec(wqt.shape, lambda s: (0, 0)), pl.BlockSpec(keys.shape, lambda s: (0, 0, 0)),
                  pl.BlockSpec(memory_space=pltpu.VMEM)],
        out_specs=[pl.BlockSpec((PEER_TB, NPAIR), cur), pl.BlockSpec((PEER_TB, NPAIR), cur),
                   pl.BlockSpec((PEER_TB, NPAIR), prev)],
        out_shape=[jax.ShapeDtypeStruct((n, NPAIR), jnp.int32), jax.ShapeDtypeStruct((n, NPAIR), F32),
                   jax.ShapeDtypeStruct((n, NPAIR), F32)],
        scratch_shapes=[pltpu.VMEM((PEER_HEADS * PEER_DKEY, PEER_TB), F32),
                        pltpu.VMEM((NPAIR, PEER_TB), jnp.int32), pltpu.VMEM((NPAIR, PEER_TB), F32),
                        pltpu.VMEM((PEER_TB, NPAIR), jnp.int32), pltpu.SMEM((PEER_TB, NPAIR), jnp.int32),
                        pltpu.SemaphoreType.DMA, gather_tile, gather_tile],
        compiler_params=pltpu.CompilerParams(dimension_semantics=("arbitrary",),
                                             vmem_limit_bytes=PEER_VMEM_LIMIT),
        name="peer_route_u",
    )(x, x, wqt, keys, tab_u)


def peer_ffn(x, w_q, sub_keys, exp_u, exp_v):
    B, T, D = x.shape
    xf = x.reshape(B * T, D)
    idx4, gate, h = peer_route_u(xf, w_q, sub_keys, pack_table(exp_u))
    out = _peer_gather_call(_peer_v_body, "peer_v", idx4, [h, gate], pack_table(exp_v), D,
                            [pltpu.VMEM((PEER_TB, NPAIR), F32)])
    return out.reshape(B, T, D)


def kernel(x, even_w_in, gla_a_w2, gla_a_b, gla_norm_g, rwkv_mu, rwkv_w0, rwkv_w2, rwkv_a0, rwkv_a2,
           rwkv_g2, rwkv_k_k, rwkv_k_a, rwkv_r_k, rwkv_lnx_g, rwkv_lnx_b, even_w_out, odd_w_in,
           odd_w_out, mix_ln_g, mix_ln_b, peer_w_q, peer_sub_keys, peer_u, peer_v, ffn_ln_g, ffn_ln_b):
    for layer in range(DEPTH):
        i = layer // 2
        if layer % 2 == 0:
            mix = even_mixer(x, even_w_in[i], gla_a_w2[i], gla_a_b[i], gla_norm_g[i], rwkv_mu[i], rwkv_w0[i],
                             rwkv_w2[i], rwkv_a0[i], rwkv_a2[i], rwkv_g2[i], rwkv_k_k[i], rwkv_k_a[i],
                             rwkv_r_k[i], rwkv_lnx_g[i], rwkv_lnx_b[i], even_w_out[i])
        else:
            mix = dsa_mixer(x, odd_w_in[i], odd_w_out[i])
        x = add_layer_norm(x, mix, mix_ln_g[layer], mix_ln_b[layer])
        ffn = peer_ffn(x, peer_w_q[layer], peer_sub_keys[layer], peer_u[layer], peer_v[layer])
        x = add_layer_norm(x, ffn, ffn_ln_g[layer], ffn_ln_b[layer])
    return x
```

```python
import functools

import jax
import jax.numpy as jnp
import numpy as np
from jax import lax
from jax.experimental import pallas as pl
from jax.experimental.pallas import tpu as pltpu

D_MODEL = 1024
BATCH = 32
SEQ = 2048
DEPTH = 4

GLA_HEADS = 4
GLA_VW = D_MODEL // 2
GLA_DV = GLA_VW // GLA_HEADS
GLA_KW = GLA_VW // 2
GLA_DK = GLA_KW // GLA_HEADS
GLA_LORA = 16
GLA_TAU = 16.0
GLA_CHUNK = 64
RWKV_W = D_MODEL - GLA_VW
RWKV_HEAD = 64
RWKV_HEADS = RWKV_W // RWKV_HEAD
RWKV_W_LORA = 32
RWKV_A_LORA = 32
RWKV_G_LORA = 96
RWKV_LNX_EPS = 64e-5
ATT_HEADS = 8
ATT_HEAD = D_MODEL // ATT_HEADS
IDX_HEADS = 8
IDX_DIM = 64
TOPK_MAX = 256
Q_BLOCK = 128
ROPE_THETA = 10000.0
NEG = -1e30
PEER_HEADS = 8
PEER_DKEY = 128
PEER_NKEYS = 128
PEER_N_EXPERTS = PEER_NKEYS * PEER_NKEYS
PEER_TOPK = 16
PEER_TOKEN_BLOCK = 128
DN_ALPHA = (2 * DEPTH) ** 0.25
DN_BETA = (8 * DEPTH) ** -0.25
LN_EPS = 1e-5
N_EVEN = (DEPTH + 1) // 2
N_ODD = DEPTH // 2

GLA_SPLITS = (GLA_KW, GLA_KW, GLA_VW, GLA_VW, GLA_LORA)
RWKV_SPLITS = (RWKV_W, RWKV_W, RWKV_W, RWKV_W_LORA, RWKV_A_LORA, RWKV_G_LORA)
GLA_IN = sum(GLA_SPLITS)
RWKV_IN = sum(RWKV_SPLITS)
EVEN_IN = GLA_IN + RWKV_IN
ODD_SPLITS = (ATT_HEADS * ATT_HEAD, ATT_HEAD, ATT_HEAD, IDX_HEADS * IDX_DIM, IDX_DIM, IDX_HEADS)
ODD_IN = sum(ODD_SPLITS)

F32 = jnp.float32


def _split(t, sizes):
    cuts = [int(c) for c in np.cumsum(sizes)[:-1]]
    return jnp.split(t, cuts, axis=-1)


LN_ROWS = 512


def _add_ln_body(x_ref, y_ref, g_ref, b_ref, o_ref):
    z = DN_ALPHA * x_ref[...] + y_ref[...]
    mu = jnp.mean(z, axis=-1, keepdims=True)
    zc = z - mu
    var = jnp.mean(zc * zc, axis=-1, keepdims=True)
    o_ref[...] = zc * lax.rsqrt(var + LN_EPS) * g_ref[...] + b_ref[...]


def add_layer_norm(x, y, g, b):
    B, T, D = x.shape
    n = B * T
    row = pl.BlockSpec((LN_ROWS, D), lambda i: (i, 0))
    vec = pl.BlockSpec((1, D), lambda i: (0, 0))
    out = pl.pallas_call(
        _add_ln_body,
        grid=(n // LN_ROWS,),
        in_specs=[row, row, vec, vec],
        out_specs=row,
        out_shape=jax.ShapeDtypeStruct((n, D), F32),
        name="add_layer_norm",
    )(x.reshape(n, D), y.reshape(n, D), g.reshape(1, D), b.reshape(1, D))
    return out.reshape(B, T, D)


def rope_tables(T, dim):
    inv = ROPE_THETA ** (-jnp.arange(0, dim, 2, dtype=F32) / dim)
    ang = jnp.arange(T, dtype=F32)[:, None] * inv[None, :]
    return jnp.cos(ang), jnp.sin(ang)


def apply_rope(x, cos, sin):
    x1, x2 = jnp.split(x, 2, axis=-1)
    return jnp.concatenate([x1 * cos - x2 * sin, x2 * cos + x1 * sin], -1).astype(x.dtype)


GLA_ROWS = 256


def _gla_body(q_ref, k_ref, v_ref, g_ref, al_ref, w2_ref, ab_ref, ng_ref, o_ref, st_ref):
    C = GLA_CHUNK
    bf = jnp.bfloat16

    @pl.when(pl.program_id(1) == 0)
    def _():
        st_ref[...] = jnp.zeros_like(st_ref)

    z = jnp.dot(al_ref[:, :GLA_LORA].astype(bf), w2_ref[...], preferred_element_type=F32) + ab_ref[...]
    la = -(jnp.maximum(-z, 0.0) + jnp.log1p(jnp.exp(-jnp.abs(z)))) / GLA_TAU
    row = lax.broadcasted_iota(jnp.int32, (C, GLA_KW), 0)
    tril = lax.broadcasted_iota(jnp.int32, (C, C), 0) >= lax.broadcasted_iota(jnp.int32, (C, C), 1)
    for n in range(q_ref.shape[0] // C):
        sl = slice(n * C, (n + 1) * C)
        b = la[sl]
        sh = 1
        while sh < C:
            b = b + jnp.where(row >= sh, pltpu.roll(b, sh, 0), 0.0)
            sh *= 2
        b_last = b[C - 1:C, :]
        k_c = k_ref[sl, :]
        q_in = q_ref[sl, :] * (GLA_DK ** -0.5) * jnp.exp(b)
        k_in = k_c * jnp.exp(-b)
        k_out = k_c * jnp.exp(b_last - b)
        dec = jnp.exp(b_last)
        for h in range(GLA_HEADS):
            ks = slice(h * GLA_DK, (h + 1) * GLA_DK)
            vs = slice(h * GLA_DV, (h + 1) * GLA_DV)
            qh = q_in[:, ks].astype(bf)
            vh = v_ref[sl, vs].astype(bf)
            a = lax.dot_general(qh, k_in[:, ks].astype(bf), (((1,), (1,)), ((), ())), preferred_element_type=F32)
            a = jnp.where(tril, a, 0.0)
            st = st_ref[h]
            o = jnp.dot(a.astype(bf), vh, preferred_element_type=F32)
            o = o + lax.dot_general(qh, st.astype(bf), (((1,), (1,)), ((), ())), preferred_element_type=F32)
            kv_t = lax.dot_general(vh, k_out[:, ks].astype(bf), (((0,), (0,)), ((), ())),
                                   preferred_element_type=F32)
            st_ref[h] = st * dec[:, ks] + kv_t
            o = o * lax.rsqrt(jnp.mean(o * o, axis=-1, keepdims=True) + LN_EPS) * ng_ref[...]
            gate = g_ref[sl, vs]
            o_ref[sl, vs] = o * (gate / (1.0 + jnp.exp(-gate)))


def gla_attention(p, a_w2, a_b, norm_g, B, T):
    R = min(GLA_ROWS, T)
    nr = T // R

    def cols(width, col0):
        return pl.BlockSpec((R, width), lambda b, i: (b * nr + i, col0 // width))

    def whole(shape):
        return pl.BlockSpec(shape, lambda b, i: (0,) * len(shape))

    return pl.pallas_call(
        _gla_body,
        grid=(B, nr),
        in_specs=[cols(GLA_KW, 0), cols(GLA_KW, GLA_KW), cols(GLA_VW, 2 * GLA_KW), cols(GLA_VW, 2 * GLA_KW + GLA_VW),
                  cols(LANES, 2 * GLA_KW + 2 * GLA_VW),
                  whole((GLA_LORA, GLA_KW)), whole((1, GLA_KW)), whole((1, GLA_DV))],
        out_specs=pl.BlockSpec((R, GLA_VW), lambda b, i: (b * nr + i, 0)),
        out_shape=jax.ShapeDtypeStruct((B * T, GLA_VW), F32),
        scratch_shapes=[pltpu.VMEM((GLA_HEADS, GLA_DV, GLA_DK), F32)],
        compiler_params=pltpu.CompilerParams(dimension_semantics=("arbitrary", "arbitrary"),
                                             vmem_limit_bytes=VMEM_LIMIT),
        name="gla_attention",
    )(p, p, p, p, p, a_w2.astype(jnp.bfloat16), a_b.reshape(1, GLA_KW), norm_g.reshape(1, GLA_DV))


SCAN_TC = 16


def _rwkv_scan_body(r_ref, w_ref, k_ref, v_ref, a_ref, b_ref, y_ref, s_ref):
    N = RWKV_HEAD

    @pl.when(pl.program_id(1) == 0)
    def _():
        s_ref[...] = jnp.zeros_like(s_ref)

    def step(tt, c):
        sa = jnp.zeros((N, LANES), F32)
        for j in range(N):
            sa = sa + s_ref[j] * a_ref[tt, j:j + 1, :]
        v_t = v_ref[tt]
        y = jnp.zeros((N, LANES), F32)
        for j in range(N):
            sj = s_ref[j] * w_ref[tt, j:j + 1, :] + sa * b_ref[tt, j:j + 1, :] + v_t * k_ref[tt, j:j + 1, :]
            s_ref[j] = sj
            y = y + sj * r_ref[tt, j:j + 1, :]
        y_ref[tt] = y
        return c

    lax.fori_loop(0, r_ref.shape[0], step, 0)


def rwkv7_scan(r, w, k, v, a, b):
    B, T, H, N = r.shape
    C = B * H
    tc = min(SCAN_TC, T)

    def to_scan(t):
        return t.astype(F32).transpose(1, 3, 0, 2).reshape(T, N, C)

    spec = pl.BlockSpec((tc, N, LANES), lambda g, i: (i, 0, g))
    y = pl.pallas_call(
        _rwkv_scan_body,
        grid=(C // LANES, T // tc),
        in_specs=[spec] * 6,
        out_specs=spec,
        out_shape=jax.ShapeDtypeStruct((T, N, C), F32),
        scratch_shapes=[pltpu.VMEM((N, N, LANES), F32)],
        compiler_params=pltpu.CompilerParams(dimension_semantics=("arbitrary", "arbitrary"),
                                             vmem_limit_bytes=VMEM_LIMIT),
        name="rwkv7_scan",
    )(*(to_scan(t) for t in (r, w, k, v, a, b)))
    return y.reshape(T, N, B, H).transpose(2, 0, 3, 1)


def even_mixer(x, w_in, gla_a_w2, gla_a_b, gla_norm_g, rwkv_mu, rwkv_w0, rwkv_w2, rwkv_a0, rwkv_a2,
               rwkv_g2, rwkv_k_k, rwkv_k_a, rwkv_r_k, rwkv_lnx_g, rwkv_lnx_b, w_out):
    B, T, _ = x.shape

    def heads(t, h):
        return t.reshape(B, T, h, -1)

    p = matmul(x.reshape(B * T, -1), w_in).reshape(B, T, -1)
    gla_p, rw_p = p[..., :GLA_IN], p[..., GLA_IN:]
    gla_out = gla_attention(p.reshape(B * T, -1), gla_a_w2, gla_a_b, gla_norm_g, B, T).reshape(B, T, GLA_VW)
    prev = jnp.pad(rw_p, ((0, 0), (1, 0), (0, 0)))[:, :-1]
    rw_p = rw_p + (prev - rw_p) * rwkv_mu
    r, k, v, wl, al, gl = _split(rw_p, RWKV_SPLITS)
    w_log = -jax.nn.softplus(-(rwkv_w0 + jnp.tanh(wl) @ rwkv_w2).astype(F32)) - 0.5
    decay = jnp.exp(-jnp.exp(w_log))
    a = jax.nn.sigmoid((rwkv_a0 + al @ rwkv_a2).astype(F32))
    g = jax.nn.sigmoid(gl) @ rwkv_g2
    kk = heads((k * rwkv_k_k).astype(F32), RWKV_HEADS)
    kk = kk / jnp.maximum(jnp.sqrt(jnp.sum(jnp.square(kk), -1, keepdims=True)), 1e-12)
    k = k * (1.0 + (a - 1.0) * rwkv_k_a)
    a_h = heads(a, RWKV_HEADS)
    r_h, k_h, v_h = heads(r, RWKV_HEADS), heads(k, RWKV_HEADS), heads(v, RWKV_HEADS)
    y = rwkv7_scan(r_h, heads(decay, RWKV_HEADS), k_h, v_h, -kk, kk * a_h)
    mu = y.mean(-1, keepdims=True)
    var = jnp.square(y - mu).mean(-1, keepdims=True)
    y = (y - mu) * lax.rsqrt(var + RWKV_LNX_EPS) * rwkv_lnx_g.reshape(RWKV_HEADS, RWKV_HEAD) \
        + rwkv_lnx_b.reshape(RWKV_HEADS, RWKV_HEAD)
    bonus = jnp.sum((r_h * k_h * rwkv_r_k).astype(F32), -1, keepdims=True) * v_h.astype(F32)
    rwkv_out = ((y + bonus).reshape(B, T, RWKV_W) * g.astype(F32)).astype(x.dtype)
    return matmul(jnp.concatenate([gla_out, rwkv_out], -1).reshape(B * T, -1), w_out).reshape(B, T, -1)


MM_ROWS = 512
VMEM_LIMIT = 48 * 1024 * 1024


def _mm_body(x_ref, w_ref, o_ref):
    o_ref[...] = jnp.dot(x_ref[...].astype(jnp.bfloat16), w_ref[...], preferred_element_type=F32)


def matmul(x, w):
    M, K = x.shape
    N = w.shape[1]
    tm = min(MM_ROWS, M)
    return pl.pallas_call(
        _mm_body,
        grid=(M // tm,),
        in_specs=[pl.BlockSpec((tm, K), lambda i: (i, 0)), pl.BlockSpec((K, N), lambda i: (0, 0))],
        out_specs=pl.BlockSpec((tm, N), lambda i: (i, 0)),
        out_shape=jax.ShapeDtypeStruct((M, N), F32),
        compiler_params=pltpu.CompilerParams(vmem_limit_bytes=VMEM_LIMIT),
        name="matmul",
    )(x, w.astype(jnp.bfloat16))


LANES = 128
INT_MIN = -2 ** 31
DSA_Q0 = 0
DSA_QI0 = ATT_HEADS * ATT_HEAD
DSA_K0 = DSA_QI0 + IDX_HEADS * IDX_DIM
DSA_V0 = DSA_K0 + ATT_HEAD
DSA_KIW0 = DSA_V0 + ATT_HEAD
DSA_COLS = DSA_KIW0 + LANES
DSA_KEY_CLASSES = 8


def _swap32(x):
    lane = lax.broadcasted_iota(jnp.int32, x.shape, 1)
    return jnp.where((lane % IDX_DIM) < IDX_DIM // 2, pltpu.roll(x, LANES - IDX_DIM // 2, 1),
                     pltpu.roll(x, IDX_DIM // 2, 1))


def _count_ge(key, cand):
    return jnp.sum(jnp.where(key >= cand, 1.0, 0.0), axis=1, keepdims=True)


def _dsa_body(n_sel, q_ref, qi_ref, kiwq_ref, k_ref, v_ref, kiw_ref, cqb_ref, sqb_ref, cib_ref, sib_ref,
              cq_ref, sq_ref, ci_ref, si_ref, o_ref, krot_ref, vbf_ref, kirot_ref):
    j = pl.program_id(1)
    T = k_ref.shape[0]
    QB = q_ref.shape[0]

    @pl.when(j == 0)
    def _():
        k = k_ref[...]
        krot_ref[...] = (k * cq_ref[...] + pltpu.roll(k, ATT_HEAD // 2, 1) * sq_ref[...]).astype(jnp.bfloat16)
        vbf_ref[...] = v_ref[...].astype(jnp.bfloat16)
        kiw = kiw_ref[...]
        kirot_ref[...] = (kiw * ci_ref[...] + _swap32(kiw) * si_ref[...]).astype(jnp.bfloat16)

    def attend(nk):
        col_bits = (nk - 1).bit_length()
        ci_b = cib_ref[...]
        si_b = sib_ref[...]
        ki = kirot_ref[:nk, :IDX_DIM]
        wi = kiwq_ref[:, IDX_DIM:IDX_DIM + IDX_HEADS] * (IDX_HEADS ** -0.5 * IDX_DIM ** -0.5)
        score = jnp.zeros((QB, nk), F32)
        for g in range(IDX_HEADS * IDX_DIM // LANES):
            qg = qi_ref[:, g * LANES:(g + 1) * LANES]
            qg = (qg * ci_b + _swap32(qg) * si_b).astype(jnp.bfloat16)
            for hh in range(LANES // IDX_DIM):
                h = g * (LANES // IDX_DIM) + hh
                s = lax.dot_general(qg[:, hh * IDX_DIM:(hh + 1) * IDX_DIM], ki, (((1,), (1,)), ((), ())),
                                    preferred_element_type=F32)
                score = score + wi[:, h:h + 1] * jnp.maximum(s, 0.0)
        col = lax.broadcasted_iota(jnp.int32, (QB, nk), 1)
        row = j * QB + lax.broadcasted_iota(jnp.int32, (QB, nk), 0)
        causal = col <= row
        score = jnp.where(causal, score, NEG)

        bits = pltpu.bitcast(score, jnp.int32)
        key = jnp.where(bits < 0, bits ^ jnp.int32(0x7FFFFFFF), bits)
        kf = float(n_sel)
        t0 = jnp.where(_count_ge(key, jnp.zeros((QB, 1), jnp.int32)) >= kf, jnp.int32(0), jnp.int32(INT_MIN))

        def bit_step(i, t):
            cand = t | (jnp.int32(1) << (30 - i))
            return jnp.where(_count_ge(key, cand) >= kf, cand, t)

        thr = lax.fori_loop(0, 31, bit_step, t0)
        gt = key > thr
        eq = key == thr
        need = kf - jnp.sum(jnp.where(gt, 1.0, 0.0), axis=1, keepdims=True)

        def tie_step(i, jj):
            cand = jj | (jnp.int32(1) << (col_bits - 1 - i))
            c = jnp.sum(jnp.where(eq & (col < cand), 1.0, 0.0), axis=1, keepdims=True)
            return jnp.where(c < need, cand, jj)

        n_eq = jnp.sum(jnp.where(eq, 1.0, 0.0), axis=1, keepdims=True)
        any_tie = jnp.max(jnp.where(n_eq > need, 1.0, 0.0)) > 0.0
        jstar = lax.cond(any_tie,
                         lambda: lax.fori_loop(0, col_bits, tie_step, jnp.zeros((QB, 1), jnp.int32)),
                         lambda: jnp.full((QB, 1), nk, jnp.int32))
        sel = (gt | (eq & (col <= jstar))) & causal
        bias = jnp.where(sel, 0.0, NEG)

        cq_b = cqb_ref[...]
        sq_b = sqb_ref[...]
        kr = krot_ref[:nk, :]
        vb = vbf_ref[:nk, :]
        for h in range(ATT_HEADS):
            qh = q_ref[:, h * ATT_HEAD:(h + 1) * ATT_HEAD]
            qh = (qh * cq_b + pltpu.roll(qh, ATT_HEAD // 2, 1) * sq_b).astype(jnp.bfloat16)
            logit = lax.dot_general(qh, kr, (((1,), (1,)), ((), ())), preferred_element_type=F32)
            logit = logit * (ATT_HEAD ** -0.5) + bias
            m = jnp.max(logit, axis=1, keepdims=True)
            p = jnp.exp(logit - m)
            l = jnp.sum(p, axis=1, keepdims=True)
            o = jnp.dot(p.astype(jnp.bfloat16), vb, preferred_element_type=F32)
            o_ref[:, h * ATT_HEAD:(h + 1) * ATT_HEAD] = o / l

    nq = T // QB
    n_cls = DSA_KEY_CLASSES if nq % DSA_KEY_CLASSES == 0 else 1
    per = nq // n_cls
    for c in range(n_cls):
        nk = max((c + 1) * per * QB, n_sel)
        pl.when((j >= c * per) & (j < (c + 1) * per))(functools.partial(attend, nk))


def dsa_attention(p, B, T):
    n_sel = min(TOPK_MAX, T // 4)
    nq = T // Q_BLOCK
    cos_a, sin_a = rope_tables(T, ATT_HEAD)
    cos_i, sin_i = rope_tables(T, IDX_DIM)
    cq = jnp.concatenate([cos_a, cos_a], -1)
    sq = jnp.concatenate([-sin_a, sin_a], -1)
    ci = jnp.concatenate([cos_i, cos_i, cos_i, cos_i], -1)
    si = jnp.concatenate([-sin_i, sin_i, -sin_i, sin_i], -1)

    def qrow(width, col0):
        return pl.BlockSpec((Q_BLOCK, width), lambda b, j: (b * nq + j, col0 // width))

    def krow(col0):
        return pl.BlockSpec((T, LANES), lambda b, j: (b, col0 // LANES))

    tab_q = pl.BlockSpec((Q_BLOCK, LANES), lambda b, j: (j, 0))
    tab_k = pl.BlockSpec((T, LANES), lambda b, j: (0, 0))
    return pl.pallas_call(
        functools.partial(_dsa_body, n_sel),
        grid=(B, nq),
        in_specs=[qrow(ATT_HEADS * ATT_HEAD, DSA_Q0), qrow(IDX_HEADS * IDX_DIM, DSA_QI0), qrow(LANES, DSA_KIW0),
                  krow(DSA_K0), krow(DSA_V0), krow(DSA_KIW0),
                  tab_q, tab_q, tab_q, tab_q, tab_k, tab_k, tab_k, tab_k],
        out_specs=pl.BlockSpec((Q_BLOCK, D_MODEL), lambda b, j: (b * nq + j, 0)),
        out_shape=jax.ShapeDtypeStruct((B * T, D_MODEL), F32),
        scratch_shapes=[pltpu.VMEM((T, ATT_HEAD), jnp.bfloat16), pltpu.VMEM((T, ATT_HEAD), jnp.bfloat16),
                        pltpu.VMEM((T, LANES), jnp.bfloat16)],
        compiler_params=pltpu.CompilerParams(dimension_semantics=("arbitrary", "arbitrary"),
                                             vmem_limit_bytes=VMEM_LIMIT),
        name="dsa_attention",
    )(p, p, p, p, p, p, cq, sq, ci, si, cq, sq, ci, si)


def dsa_mixer(x, w_in, w_out):
    B, T, D = x.shape
    wq, wk, wv, wqi, wki, wwi = _split(w_in, ODD_SPLITS)
    pad = jnp.zeros((D, DSA_COLS - DSA_KIW0 - IDX_DIM - IDX_HEADS), w_in.dtype)
    w_cat = jnp.concatenate([wq, wqi, wk, wv, wki, wwi, pad], axis=1)
    p = matmul(x.reshape(B * T, D), w_cat)
    o = dsa_attention(p, B, T)
    return matmul(o, w_out).reshape(B, T, D)


NPAIR = PEER_HEADS * PEER_TOPK
PEER_TB = 128
SLAB = 4
GS = 136
PAIRS_PER_ITER = 8
V_PAIRS_PER_ITER = 16
GATHER_CHUNKS = 4
PEER_VMEM_LIMIT = 56 * 1024 * 1024
NEG_INF = float("-inf")


def pack_table(tab):
    E = tab.shape[0]
    b = lax.bitcast_convert_type(tab.astype(jnp.bfloat16), jnp.uint16).astype(jnp.uint32)
    b = b.reshape(E, SLAB, 2, LANES)
    w = b[:, :, 0, :] | (b[:, :, 1, :] << 16)
    return lax.bitcast_convert_type(w, jnp.int32).reshape(E * SLAB, LANES)


def _gather_rows(idx_ref, base, tab_ref, gt_ref):
    for p in range(NPAIR):
        i = pl.multiple_of(idx_ref[0, 0, base + p], SLAB)
        gt_ref[pl.ds(p, SLAB, stride=GS), :] = tab_ref[pl.ds(i, SLAB), :]


def _unpacked(gt_ref):
    parts = []
    for j in range(SLAB):
        w = gt_ref[pl.ds(j * GS, NPAIR), :]
        parts.append(pltpu.bitcast(w << 16, F32).astype(jnp.bfloat16))
        parts.append(pltpu.bitcast(w & jnp.int32(-65536), F32).astype(jnp.bfloat16))
    return jnp.concatenate(parts, axis=1)


def _token_loop(idx_ref, tab_ref, gta_ref, gtb_ref, compute, pairs_per_iter):
    _gather_rows(idx_ref, 0, tab_ref, gta_ref)

    def pairs(i, c):
        for k in range(pairs_per_iter):
            t = 2 * (i * pairs_per_iter + k)
            _gather_rows(idx_ref, (t + 1) * NPAIR, tab_ref, gtb_ref)
            compute(t, _unpacked(gta_ref))
            _gather_rows(idx_ref, jnp.minimum(t + 2, PEER_TB - 1) * NPAIR, tab_ref, gta_ref)
            compute(t + 1, _unpacked(gtb_ref))
        return c

    lax.fori_loop(0, PEER_TB // (2 * pairs_per_iter), pairs, 0)


def _peer_v_body(idx_ref, h_ref, gate_ref, x_ref, g_ref, b_ref, tab_ref, o_ref, coef_ref, gta_ref, gtb_ref):
    hv = h_ref[...]
    coef_ref[...] = gate_ref[...] * (0.5 * hv * (1.0 + lax.erf(hv * (2.0 ** -0.5))))

    def compute(t, g):
        ct = coef_ref[pl.ds(t, 1), :].astype(jnp.bfloat16)
        o_ref[pl.ds(t, 1), :] = jnp.dot(ct, g, preferred_element_type=F32)

    _token_loop(idx_ref, tab_ref, gta_ref, gtb_ref, compute, V_PAIRS_PER_ITER)
    z = DN_ALPHA * x_ref[...] + o_ref[...]
    mu = jnp.mean(z, axis=-1, keepdims=True)
    zc = z - mu
    var = jnp.mean(zc * zc, axis=-1, keepdims=True)
    o_ref[...] = zc * lax.rsqrt(var + LN_EPS) * g_ref[...] + b_ref[...]


def _peer_gather_call(body, name, idx4, acts, consts, tab, out_w, extra_scratch):
    n = acts[0].shape[0]
    nb = n // PEER_TB
    gather_tile = pltpu.VMEM((SLAB * GS, LANES), jnp.int32)
    return pl.pallas_call(
        body,
        grid=(nb,),
        in_specs=[pl.BlockSpec((1, 1, PEER_TB * NPAIR), lambda i: (i, 0, 0), memory_space=pltpu.SMEM)]
        + [pl.BlockSpec((PEER_TB, a.shape[1]), lambda i: (i, 0)) for a in acts]
        + [pl.BlockSpec(c.shape, lambda i: (0, 0)) for c in consts]
        + [pl.BlockSpec(memory_space=pltpu.VMEM)],
        out_specs=pl.BlockSpec((PEER_TB, out_w), lambda i: (i, 0)),
        out_shape=jax.ShapeDtypeStruct((n, out_w), F32),
        scratch_shapes=extra_scratch + [gather_tile, gather_tile],
        compiler_params=pltpu.CompilerParams(dimension_semantics=("arbitrary",),
                                             vmem_limit_bytes=PEER_VMEM_LIMIT),
        name=name,
    )(idx4.reshape(nb, 1, PEER_TB * NPAIR), *acts, *consts, tab)


class _Stepper:
    def __init__(self, gen):
        self.gen, self.done, self.value = gen, False, None

    def step(self, n=1):
        for _ in range(n):
            if self.done:
                return
            try:
                next(self.gen)
            except StopIteration as e:
                self.done, self.value = True, e.value

    def finish(self):
        while not self.done:
            self.step()
        return self.value


def _top16_steps(vals, ids):
    n, T = vals.shape
    pos = lax.broadcasted_iota(jnp.int32, (n, T), 0).astype(F32)
    slot = lax.broadcasted_iota(jnp.int32, (PEER_TOPK, T), 0)
    top_v = jnp.zeros((PEER_TOPK, T), F32)
    top_i = jnp.zeros((PEER_TOPK, T), F32)
    for r in range(PEER_TOPK):
        m = jnp.max(vals, axis=0, keepdims=True)
        am = jnp.min(jnp.where(vals == m, pos, float(n)), axis=0, keepdims=True)
        hit = pos == am
        sel = am if ids is None else jnp.max(jnp.where(hit, ids, -1.0), axis=0, keepdims=True)
        top_v = jnp.where(slot == r, m, top_v)
        top_i = jnp.where(slot == r, sel, top_i)
        vals = jnp.where(hit, NEG_INF, vals)
        yield
    return top_v, top_i


def _route_head_steps(keys0, keys1, q0, q1):
    T = q0.shape[1]
    s0, i0 = yield from _top16_steps(jnp.dot(keys0, q0.astype(jnp.bfloat16), preferred_element_type=F32), None)
    s1, i1 = yield from _top16_steps(jnp.dot(keys1, q1.astype(jnp.bfloat16), preferred_element_type=F32), None)
    sub8 = lax.broadcasted_iota(jnp.int32, (8, T), 0)
    cs, cids = [s0[0:1, :] + s1], [i0[0:1, :] * PEER_NKEYS + i1]
    for a in range(1, 8):
        keep = sub8 < PEER_TOPK // (a + 1)
        cs.append(jnp.where(keep, s0[a:a + 1, :] + s1[0:8, :], NEG_INF))
        cids.append(i0[a:a + 1, :] * PEER_NKEYS + i1[0:8, :])
    cs.append(s0[8:16, :] + s1[0:1, :])
    cids.append(i0[8:16, :] * PEER_NKEYS + i1[0:1, :])
    best_s, best_i = yield from _top16_steps(jnp.concatenate(cs, axis=0), jnp.concatenate(cids, axis=0))
    e = jnp.exp(best_s - best_s[0:1, :])
    return (best_i * SLAB).astype(jnp.int32), e / jnp.sum(e, axis=0, keepdims=True)


def _route_u_body(xr_ref, xg_ref, wqt_ref, keys_ref, tab_ref, idx_ref, gate_ref, h_ref,
                  qt_ref, idxt_ref, gatet_ref, stage_ref, idx_smem, sem, gta_ref, gtb_ref):
    s = pl.program_id(0)
    half = PEER_DKEY // 2

    def to_smem():
        cp = pltpu.make_async_copy(stage_ref, idx_smem, sem)
        cp.start()
        cp.wait()

    @pl.when(s == 0)
    def _():
        stage_ref[...] = jnp.zeros_like(stage_ref)
        to_smem()

    qt_ref[...] = lax.dot_general(wqt_ref[...], xr_ref[...].astype(jnp.bfloat16), (((1,), (1,)), ((), ())),
                                  preferred_element_type=F32)

    def gather(t, gt_ref, lo=0, hi=NPAIR):
        for p in range(lo, hi):
            i = pl.multiple_of(idx_smem[t, p], SLAB)
            gt_ref[pl.ds(p, SLAB, stride=GS), :] = tab_ref[pl.ds(i, SLAB), :]

    def compute(t, g):
        xt = xg_ref[pl.ds(t, 1), :].astype(jnp.bfloat16)
        h_ref[pl.ds(t, 1), :] = lax.dot_general(xt, g, (((1,), (1,)), ((), ())), preferred_element_type=F32)

    gather(0, gta_ref)

    def iteration(it, c):
        q0 = qt_ref[pl.ds(pl.multiple_of(2 * it * half, half), half), :]
        q1 = qt_ref[pl.ds(pl.multiple_of((2 * it + 1) * half, half), half), :]
        route = _Stepper(_route_head_steps(keys_ref[2 * it], keys_ref[2 * it + 1], q0, q1))
        chunk = NPAIR // GATHER_CHUNKS
        for k in range(PAIRS_PER_ITER):
            t = 2 * (it * PAIRS_PER_ITER + k)
            for c4 in range(GATHER_CHUNKS):
                gather(t + 1, gtb_ref, c4 * chunk, (c4 + 1) * chunk)
                route.step()
            compute(t, _unpacked(gta_ref))
            nxt = jnp.minimum(t + 2, PEER_TB - 1)
            for c4 in range(GATHER_CHUNKS):
                gather(nxt, gta_ref, c4 * chunk, (c4 + 1) * chunk)
                route.step()
            compute(t + 1, _unpacked(gtb_ref))
        i_h, g_h = route.finish()
        rows = pl.ds(pl.multiple_of(it * PEER_TOPK, PEER_TOPK), PEER_TOPK)
        idxt_ref[rows, :] = i_h
        gatet_ref[rows, :] = g_h
        return c

    lax.fori_loop(0, PEER_HEADS, iteration, 0)
    idx_tok = idxt_ref[...].T
    idx_ref[...] = idx_tok
    gate_ref[...] = gatet_ref[...].T
    stage_ref[...] = idx_tok
    to_smem()


def peer_route_u(x, w_q, sub_keys, tab_u):
    n = x.shape[0]
    nb = n // PEER_TB
    assert PEER_TB == 2 * PAIRS_PER_ITER * PEER_HEADS and NPAIR == PEER_TB
    wqt = w_q.T.astype(jnp.bfloat16)
    keys = sub_keys.reshape(PEER_HEADS * 2, PEER_NKEYS, PEER_DKEY // 2).astype(jnp.bfloat16)
    cur = lambda s: (jnp.minimum(s, nb - 1), 0)
    prev = lambda s: (jnp.maximum(s - 1, 0), 0)
    gather_tile = pltpu.VMEM((SLAB * GS, LANES), jnp.int32)
    return pl.pallas_call(
        _route_u_body,
        grid=(nb + 1,),
        in_specs=[pl.BlockSpec((PEER_TB, D_MODEL), cur), pl.BlockSpec((PEER_TB, D_MODEL), prev),
                  pl.BlockSpec(wqt.shape, lambda s: (0, 0)), pl.BlockSpec(keys.shape, lambda s: (0, 0, 0)),
                  pl.BlockSpec(memory_space=pltpu.VMEM)],
        out_specs=[pl.BlockSpec((PEER_TB, NPAIR), cur), pl.BlockSpec((PEER_TB, NPAIR), cur),
                   pl.BlockSpec((PEER_TB, NPAIR), prev)],
        out_shape=[jax.ShapeDtypeStruct((n, NPAIR), jnp.int32), jax.ShapeDtypeStruct((n, NPAIR), F32),
                   jax.ShapeDtypeStruct((n, NPAIR), F32)],
        scratch_shapes=[pltpu.VMEM((PEER_HEADS * PEER_DKEY, PEER_TB), F32),
                        pltpu.VMEM((NPAIR, PEER_TB), jnp.int32), pltpu.VMEM((NPAIR, PEER_TB), F32),
                        pltpu.VMEM((PEER_TB, NPAIR), jnp.int32), pltpu.SMEM((PEER_TB, NPAIR), jnp.int32),
                        pltpu.SemaphoreType.DMA, gather_tile, gather_tile],
        compiler_params=pltpu.CompilerParams(dimension_semantics=("arbitrary",),
                                             vmem_limit_bytes=PEER_VMEM_LIMIT),
        name="peer_route_u",
    )(x, x, wqt, keys, tab_u)


def peer_ffn(x, w_q, sub_keys, exp_u, exp_v, ln_g, ln_b):
    B, T, D = x.shape
    xf = x.reshape(B * T, D)
    idx4, gate, h = peer_route_u(xf, w_q, sub_keys, pack_table(exp_u))
    out = _peer_gather_call(_peer_v_body, "peer_v", idx4, [h, gate, xf], [ln_g.reshape(1, D), ln_b.reshape(1, D)],
                            pack_table(exp_v), D, [pltpu.VMEM((PEER_TB, NPAIR), F32)])
    return out.reshape(B, T, D)


def kernel(x, even_w_in, gla_a_w2, gla_a_b, gla_norm_g, rwkv_mu, rwkv_w0, rwkv_w2, rwkv_a0, rwkv_a2,
           rwkv_g2, rwkv_k_k, rwkv_k_a, rwkv_r_k, rwkv_lnx_g, rwkv_lnx_b, even_w_out, odd_w_in,
           odd_w_out, mix_ln_g, mix_ln_b, peer_w_q, peer_sub_keys, peer_u, peer_v, ffn_ln_g, ffn_ln_b):
    for layer in range(DEPTH):
        i = layer // 2
        if layer % 2 == 0:
            mix = even_mixer(x, even_w_in[i], gla_a_w2[i], gla_a_b[i], gla_norm_g[i], rwkv_mu[i], rwkv_w0[i],
                             rwkv_w2[i], rwkv_a0[i], rwkv_a2[i], rwkv_g2[i], rwkv_k_k[i], rwkv_k_a[i],
                             rwkv_r_k[i], rwkv_lnx_g[i], rwkv_lnx_b[i], even_w_out[i])
        else:
            mix = dsa_mixer(x, odd_w_in[i], odd_w_out[i])
        x = add_layer_norm(x, mix, mix_ln_g[layer], mix_ln_b[layer])
        x = peer_ffn(x, peer_w_q[layer], peer_sub_keys[layer], peer_u[layer], peer_v[layer],
                     ffn_ln_g[layer], ffn_ln_b[layer])
    return x
```
